```python
import math
import jax, jax.numpy as jnp
from jax import lax
import numpy as np

D_MODEL = 1024
BATCH = 8
SEQ = 4096
DEPTH = 2
DEC_BATCH = 32
DEC_SEQ = 4
PAST_LEN = 16384
PAGE_SIZE = 128

D_MIX = D_MODEL
LRU_WIDTH = D_MIX // 4
LRU_BLOCKS = 4
LRU_BW = LRU_WIDTH // LRU_BLOCKS
LRU_C = 8.0
SB_HEADS = 8
SB_HEAD_DIM = 64
SB_WIDTH = SB_HEADS * SB_HEAD_DIM
SB_BLOCK = 128
SB_BIAS_INIT = -8.0
SSD_HEADS = 4
SSD_HEAD_DIM = 64
SSD_D_INNER = SSD_HEADS * SSD_HEAD_DIM
SSD_GROUPS = 2
SSD_D_STATE = 64
SSD_CONV_DIM = SSD_D_INNER + 2 * SSD_GROUPS * SSD_D_STATE
SSD_CHUNK = 128
CONV_W = 4
D_FF = 2816
N_EXPERTS = 8
TOP_K = 2
MOE_D_FF = 3584
N_DENSE = (DEPTH + 1) // 2
N_MOE = DEPTH // 2
PLE_DIM = 256
DN_ALPHA = (2 * DEPTH) ** 0.25
DN_BETA = (8 * DEPTH) ** -0.25
LN_EPS = 1e-5
RMS_EPS = 1e-6
IN_SIZES = (LRU_WIDTH, LRU_WIDTH, SB_WIDTH, SB_WIDTH, SB_WIDTH, SSD_D_INNER, SSD_CONV_DIM, SSD_HEADS)
IN_COLS = sum(IN_SIZES)
IN_OFFSETS = tuple(int(s) for s in np.cumsum(IN_SIZES)[:-1])

kernel_name = 'hymba_hawk_stickbreak_ssd_decoder'


def layer_norm(x, g, b):
    xf = x.astype(jnp.float32)
    mu = jnp.mean(xf, axis=-1, keepdims=True)
    var = jnp.mean(jnp.square(xf - mu), axis=-1, keepdims=True)
    return ((xf - mu) * lax.rsqrt(var + LN_EPS) * g + b).astype(x.dtype)


def rms_norm(x, w):
    xf = x.astype(jnp.float32)
    return xf * lax.rsqrt(jnp.mean(jnp.square(xf), axis=-1, keepdims=True) + RMS_EPS) * w


def causal_conv(u, buf, w, b):
    L = u.shape[1]
    full = jnp.concatenate([buf.astype(u.dtype), u], axis=1)
    out = b + sum(full[:, j:j + L] * w[j] for j in range(CONV_W))
    return out, full[:, L:]


def block_diag(x, w):
    bsz, L, _ = x.shape
    y = jnp.einsum('blgi,gij->blgj', x.reshape(bsz, L, LRU_BLOCKS, LRU_BW), w)
    return y.reshape(bsz, L, LRU_WIDTH)


def lru_scan(a, u, h0):
    def combine(c1, c2):
        a1, u1 = c1
        a2, u2 = c2
        return a1 * a2, a2 * u1 + u2
    a_cum, u_cum = lax.associative_scan(combine, (a, u), axis=1)
    hs = u_cum + a_cum * h0[:, None, :]
    return hs, hs[:, -1]


def segsum(x):
    T = x.shape[-1]
    xr = jnp.broadcast_to(x[..., :, None], x.shape + (T,))
    strict = jnp.tril(jnp.ones((T, T), dtype=bool), -1)
    cs = jnp.cumsum(jnp.where(strict, xr, 0.0), axis=-2)
    return jnp.where(jnp.tril(jnp.ones((T, T), dtype=bool)), cs, -jnp.inf)


def ssd_chunked(x, dt, A, B, C, s0):
    bsz, L, H, P = x.shape
    Q = min(SSD_CHUNK, L)
    pad = (-L) % Q
    if pad:
        padw = lambda t: jnp.pad(t, [(0, 0), (0, pad)] + [(0, 0)] * (t.ndim - 2))
        x, dt, B, C = padw(x), padw(dt), padw(B), padw(C)
    nc = x.shape[1] // Q
    ch = lambda t: t.reshape((bsz, nc, Q) + t.shape[2:])
    xd = ch(x * dt[..., None])
    B, C = ch(B), ch(C)
    a = jnp.transpose(ch(dt * A), (0, 3, 1, 2))
    a_cs = jnp.cumsum(a, axis=-1)
    y_diag = jnp.einsum('bclhn,bcshn,bhcls,bcshp->bclhp', C, B, jnp.exp(segsum(a)), xd)
    decay_states = jnp.exp(a_cs[..., -1:] - a_cs)
    states = jnp.einsum('bclhn,bhcl,bclhp->bchpn', B, decay_states, xd)
    states = jnp.concatenate([s0[:, None], states], axis=1)
    decay_chunk = jnp.exp(segsum(jnp.pad(a_cs[..., -1], ((0, 0), (0, 0), (1, 0)))))
    new_states = jnp.einsum('bhzc,bchpn->bzhpn', decay_chunk, states)
    y_off = jnp.einsum('bclhn,bchpn,bhcl->bclhp', C, new_states[:, :-1], jnp.exp(a_cs))
    y = (y_diag + y_off).reshape(bsz, nc * Q, H, P)[:, :L]
    return y, new_states[:, -1]


def sb_weights(z, mask):
    log_beta = jax.nn.log_sigmoid(z)
    log_rest = jnp.where(mask, jax.nn.log_sigmoid(-z), 0.0)
    later = lax.cumsum(log_rest, axis=z.ndim - 1, reverse=True) - log_rest
    return jnp.where(mask, jnp.exp(log_beta + later), 0.0)


def sb_prompt(q, k, v, bias):
    bsz, S, H, d = q.shape
    nb = S // SB_BLOCK
    qb = jnp.moveaxis(q.reshape(bsz, nb, SB_BLOCK, H, d), 1, 0)
    k_pos = jnp.arange(S)
    scale = 1.0 / math.sqrt(d)
    bias_f = bias.astype(jnp.float32)[None, :, None, None]

    def block(args):
        qi, bi = args
        z = jnp.einsum('bqhd,bkhd->bhqk', qi, k, preferred_element_type=jnp.float32) * scale + bias_f
        q_pos = bi * SB_BLOCK + jnp.arange(SB_BLOCK)
        A = sb_weights(z, k_pos[None, :] < q_pos[:, None])
        return jnp.einsum('bhqk,bkhd->bqhd', A.astype(v.dtype), v)

    out = lax.map(block, (qb, jnp.arange(nb)))
    return jnp.moveaxis(out, 0, 1).reshape(bsz, S, H, d)


def sb_sample(q, k, v, k_past, v_past, bias):
    T = q.shape[1]
    P = k_past.shape[1]
    scale = 1.0 / math.sqrt(q.shape[-1])
    z = jnp.concatenate([
        jnp.einsum('bqhd,bkhd->bhqk', q, k_past, preferred_element_type=jnp.float32),
        jnp.einsum('bqhd,bkhd->bhqk', q, k, preferred_element_type=jnp.float32)], axis=-1) * scale
    z = z + bias.astype(jnp.float32)[None, :, None, None]
    q_pos = P + jnp.arange(T)
    k_pos = jnp.arange(P + T)
    A = sb_weights(z, k_pos[None, :] < q_pos[:, None]).astype(v.dtype)
    return (jnp.einsum('bhqk,bkhd->bqhd', A[..., :P], v_past)
            + jnp.einsum('bhqk,bkhd->bqhd', A[..., P:], v))


def mixer(h, start_pos, lru_h0, lru_buf, ssd_s0, ssd_buf, sb_fn, w_in, lru_conv_w, lru_conv_b,
          lru_wa, lru_ba, lru_wx, lru_bx, lru_lambda, sb_bias, ssd_conv_w, ssd_conv_b, ssd_dt_bias,
          ssd_a_log, ssd_d, ssd_norm_w, w_out):
    f32 = jnp.float32
    bsz, L, _ = h.shape
    proj = h @ w_in
    x_lru, g_lru, q, k, v, z, xbc_raw, dt_raw = jnp.split(proj, IN_OFFSETS, axis=-1)

    xc, lru_buf_new = causal_conv(x_lru, lru_buf, lru_conv_w, lru_conv_b)
    r = jax.nn.sigmoid(block_diag(xc, lru_wa).astype(f32) + lru_ba)
    i = jax.nn.sigmoid(block_diag(xc, lru_wx).astype(f32) + lru_bx)
    log_a = -LRU_C * r * jax.nn.softplus(-lru_lambda.astype(f32))
    mult = jnp.sqrt(-jnp.expm1(2.0 * log_a))
    first = (start_pos + jnp.arange(L)) == 0
    mult = jnp.where(first[None, :, None], 1.0, mult)
    hs, lru_hT = lru_scan(jnp.exp(log_a), mult * i * xc.astype(f32), lru_h0.astype(f32))
    y_lru = hs * jax.nn.gelu(g_lru.astype(f32))

    q = q.reshape(bsz, L, SB_HEADS, SB_HEAD_DIM)
    k = k.reshape(bsz, L, SB_HEADS, SB_HEAD_DIM)
    v = v.reshape(bsz, L, SB_HEADS, SB_HEAD_DIM)
    y_sb = sb_fn(q, k, v, sb_bias).reshape(bsz, L, SB_WIDTH).astype(f32)

    xbc, ssd_buf_new = causal_conv(xbc_raw, ssd_buf, ssd_conv_w, ssd_conv_b)
    xbc = jax.nn.silu(xbc.astype(f32))
    xs, Bm, Cm = jnp.split(xbc, [SSD_D_INNER, SSD_D_INNER + SSD_GROUPS * SSD_D_STATE], axis=-1)
    xs = xs.reshape(bsz, L, SSD_HEADS, SSD_HEAD_DIM)
    rep = SSD_HEADS // SSD_GROUPS
    Bm = jnp.repeat(Bm.reshape(bsz, L, SSD_GROUPS, SSD_D_STATE), rep, axis=2)
    Cm = jnp.repeat(Cm.reshape(bsz, L, SSD_GROUPS, SSD_D_STATE), rep, axis=2)
    dt = jax.nn.softplus(dt_raw.astype(f32) + ssd_dt_bias)
    A = -jnp.exp(ssd_a_log.astype(f32))
    y, ssd_sT = ssd_chunked(xs, dt, A, Bm, Cm, ssd_s0.astype(f32))
    y = (y + ssd_d[:, None].astype(f32) * xs).reshape(bsz, L, SSD_D_INNER) * jax.nn.silu(z.astype(f32))
    y_ssd = rms_norm(y, ssd_norm_w)

    mixed = jnp.concatenate([y_lru, y_sb, y_ssd], axis=-1).astype(h.dtype)
    return mixed @ w_out, (k, v, lru_hT, lru_buf_new, ssd_sT, ssd_buf_new)


def swiglu(x, wg, wu, wd):
    return (jax.nn.silu(x @ wg) * (x @ wu)) @ wd


def moe_swiglu(x, router, wg, wu, wd):
    shp = x.shape
    t = x.reshape(-1, shp[-1])
    logits = (t @ router).astype(jnp.float32)
    vals, idx = lax.top_k(logits, TOP_K)
    w = jax.nn.softmax(vals, axis=-1)
    gates = jnp.sum(jax.nn.one_hot(idx, N_EXPERTS, dtype=jnp.float32) * w[..., None], axis=1)
    out = jnp.zeros_like(t)
    for e in range(N_EXPERTS):
        out = out + gates[:, e:e + 1].astype(t.dtype) * swiglu(t, wg[e], wu[e], wd[e])
    return out.reshape(shp)


def run_group(x, p, start_pos, lru_h0, lru_buf0, ssd_s0, ssd_buf0, sb_for_layer, W):
    x = layer_norm(x, W['ln_in_g'], W['ln_in_b'])
    new_states = []
    for li in range(DEPTH):
        mix, st = mixer(x, start_pos, lru_h0[li], lru_buf0[li], ssd_s0[li], ssd_buf0[li], sb_for_layer(li),
                        W['w_in'][li], W['lru_conv_w'][li], W['lru_conv_b'][li], W['lru_wa'][li],
                        W['lru_ba'][li], W['lru_wx'][li], W['lru_bx'][li], W['lru_lambda'][li],
                        W['sb_bias'][li], W['ssd_conv_w'][li], W['ssd_conv_b'][li], W['ssd_dt_bias'][li],
                        W['ssd_a_log'][li], W['ssd_d'][li], W['ssd_norm_w'][li], W['w_out'][li])
        x1 = layer_norm(DN_ALPHA * x + mix, W['ln1_g'][li], W['ln1_b'][li])
        j = li // 2
        if li % 2 == 0:
            f = swiglu(x1, W['ffn_w_gate'][j], W['ffn_w_up'][j], W['ffn_w_down'][j])
        else:
            f = moe_swiglu(x1, W['moe_router'][j], W['moe_w_gate'][j], W['moe_w_up'][j], W['moe_w_down'][j])
        ple = jax.nn.sigmoid(x1 @ W['ple_w_gate'][li]) * (p[li] @ W['ple_w_proj'][li])
        x = layer_norm(DN_ALPHA * x1 + f + ple, W['ln2_g'][li], W['ln2_b'][li])
        new_states.append(st)
    stacked = tuple(jnp.stack([st[n] for st in new_states]).astype(x.dtype) for n in range(6))
    return x, stacked


def setup_inputs(seed: int = 0) -> dict:
    key = jax.random.key(seed)
    ks = iter(jax.random.split(key, 64))
    f32 = jnp.float32

    def nrm(shape, scale=1.0):
        return jax.random.normal(next(ks), shape, f32) * scale

    def unif(shape, lo, hi):
        return jax.random.uniform(next(ks), shape, f32, lo, hi)

    n_pages = PAST_LEN // PAGE_SIZE
    n_pool = (DEC_BATCH * n_pages * 5) // 4
    page_table = jax.random.permutation(next(ks), n_pool)[:DEC_BATCH * n_pages]
    page_table = page_table.reshape(DEC_BATCH, n_pages).astype(jnp.int32)

    a_lru = unif((DEPTH, LRU_WIDTH), 0.9, 0.999) ** (1.0 / LRU_C)
    dt0 = jnp.exp(unif((DEPTH, SSD_HEADS), math.log(1e-3), math.log(1e-1)))
    dm = D_MODEL ** -0.5
    return {
        'x_prompt': nrm((BATCH, SEQ, D_MODEL)),
        'x_sample': nrm((DEC_BATCH, DEC_SEQ, D_MODEL)),
        'p_prompt': nrm((DEPTH, BATCH, SEQ, PLE_DIM)),
        'p_sample': nrm((DEPTH, DEC_BATCH, DEC_SEQ, PLE_DIM)),
        'cache_k': nrm((DEPTH, n_pool, PAGE_SIZE, SB_HEADS, SB_HEAD_DIM)),
        'cache_v': nrm((DEPTH, n_pool, PAGE_SIZE, SB_HEADS, SB_HEAD_DIM)),
        'page_table': page_table,
        'state_lru_h': nrm((DEPTH, DEC_BATCH, LRU_WIDTH), 0.5),
        'state_lru_conv': nrm((DEPTH, DEC_BATCH, CONV_W - 1, LRU_WIDTH)),
        'state_ssd': nrm((DEPTH, DEC_BATCH, SSD_HEADS, SSD_HEAD_DIM, SSD_D_STATE), 0.1),
        'state_ssd_conv': nrm((DEPTH, DEC_BATCH, CONV_W - 1, SSD_CONV_DIM)),
        'ln_in_g': 1.0 + nrm((D_MODEL,), 0.02),
        'ln_in_b': nrm((D_MODEL,), 0.01),
        'w_in': nrm((DEPTH, D_MODEL, IN_COLS), dm),
        'lru_conv_w': nrm((DEPTH, CONV_W, LRU_WIDTH), CONV_W ** -0.5),
        'lru_conv_b': nrm((DEPTH, LRU_WIDTH), 0.01),
        'lru_wa': nrm((DEPTH, LRU_BLOCKS, LRU_BW, LRU_BW), LRU_BW ** -0.5),
        'lru_ba': nrm((DEPTH, LRU_WIDTH), 0.01),
        'lru_wx': nrm((DEPTH, LRU_BLOCKS, LRU_BW, LRU_BW), LRU_BW ** -0.5),
        'lru_bx': nrm((DEPTH, LRU_WIDTH), 0.01),
        'lru_lambda': jnp.log(a_lru) - jnp.log1p(-a_lru),
        'sb_bias': SB_BIAS_INIT + nrm((DEPTH, SB_HEADS), 0.1),
        'ssd_conv_w': nrm((DEPTH, CONV_W, SSD_CONV_DIM), CONV_W ** -0.5),
        'ssd_conv_b': nrm((DEPTH, SSD_CONV_DIM), 0.01),
        'ssd_dt_bias': dt0 + jnp.log(-jnp.expm1(-dt0)),
        'ssd_a_log': jnp.log(unif((DEPTH, SSD_HEADS), 1.0, 16.0)),
        'ssd_d': 1.0 + nrm((DEPTH, SSD_HEADS), 0.02),
        'ssd_norm_w': 1.0 + nrm((DEPTH, SSD_D_INNER), 0.02),
        'w_out': nrm((DEPTH, D_MIX, D_MODEL), D_MIX ** -0.5 * DN_BETA),
        'ln1_g': 1.0 + nrm((DEPTH, D_MODEL), 0.02),
        'ln1_b': nrm((DEPTH, D_MODEL), 0.01),
        'ln2_g': 1.0 + nrm((DEPTH, D_MODEL), 0.02),
        'ln2_b': nrm((DEPTH, D_MODEL), 0.01),
        'ffn_w_gate': nrm((N_DENSE, D_MODEL, D_FF), dm),
        'ffn_w_up': nrm((N_DENSE, D_MODEL, D_FF), dm),
        'ffn_w_down': nrm((N_DENSE, D_FF, D_MODEL), D_FF ** -0.5 * DN_BETA),
        'moe_router': nrm((N_MOE, D_MODEL, N_EXPERTS), dm),
        'moe_w_gate': nrm((N_MOE, N_EXPERTS, D_MODEL, MOE_D_FF), dm),
        'moe_w_up': nrm((N_MOE, N_EXPERTS, D_MODEL, MOE_D_FF), dm),
        'moe_w_down': nrm((N_MOE, N_EXPERTS, MOE_D_FF, D_MODEL), MOE_D_FF ** -0.5 * DN_BETA),
        'ple_w_gate': nrm((DEPTH, D_MODEL, D_MODEL), dm),
        'ple_w_proj': nrm((DEPTH, PLE_DIM, D_MODEL), PLE_DIM ** -0.5 * DN_BETA),
    }


def reference(x_prompt, x_sample, p_prompt, p_sample, cache_k, cache_v, page_table, state_lru_h,
              state_lru_conv, state_ssd, state_ssd_conv, ln_in_g, ln_in_b, w_in, lru_conv_w, lru_conv_b,
              lru_wa, lru_ba, lru_wx, lru_bx, lru_lambda, sb_bias, ssd_conv_w, ssd_conv_b, ssd_dt_bias,
              ssd_a_log, ssd_d, ssd_norm_w, w_out, ln1_g, ln1_b, ln2_g, ln2_b, ffn_w_gate, ffn_w_up,
              ffn_w_down, moe_router, moe_w_gate, moe_w_up, moe_w_down, ple_w_gate, ple_w_proj):
    W = dict(ln_in_g=ln_in_g, ln_in_b=ln_in_b, w_in=w_in, lru_conv_w=lru_conv_w, lru_conv_b=lru_conv_b,
             lru_wa=lru_wa, lru_ba=lru_ba, lru_wx=lru_wx, lru_bx=lru_bx, lru_lambda=lru_lambda,
             sb_bias=sb_bias, ssd_conv_w=ssd_conv_w, ssd_conv_b=ssd_conv_b, ssd_dt_bias=ssd_dt_bias,
             ssd_a_log=ssd_a_log, ssd_d=ssd_d, ssd_norm_w=ssd_norm_w, w_out=w_out, ln1_g=ln1_g,
             ln1_b=ln1_b, ln2_g=ln2_g, ln2_b=ln2_b, ffn_w_gate=ffn_w_gate, ffn_w_up=ffn_w_up,
             ffn_w_down=ffn_w_down, moe_router=moe_router, moe_w_gate=moe_w_gate, moe_w_up=moe_w_up,
             moe_w_down=moe_w_down, ple_w_gate=ple_w_gate, ple_w_proj=ple_w_proj)

    bp = x_prompt.shape[0]
    dtp = x_prompt.dtype
    z_lru_h = jnp.zeros((DEPTH, bp, LRU_WIDTH), dtp)
    z_lru_c = jnp.zeros((DEPTH, bp, CONV_W - 1, LRU_WIDTH), dtp)
    z_ssd = jnp.zeros((DEPTH, bp, SSD_HEADS, SSD_HEAD_DIM, SSD_D_STATE), dtp)
    z_ssd_c = jnp.zeros((DEPTH, bp, CONV_W - 1, SSD_CONV_DIM), dtp)
    y_prompt, (k_p, v_p, lh_p, lc_p, s_p, sc_p) = run_group(
        x_prompt, p_prompt, 0, z_lru_h, z_lru_c, z_ssd, z_ssd_c, lambda li: sb_prompt, W)

    n_seq, n_pages = page_table.shape
    past = n_pages * PAGE_SIZE

    def sample_sb(li):
        def fn(q, k, v, bias):
            kp = jnp.take(cache_k[li], page_table, axis=0).reshape(n_seq, past, SB_HEADS, SB_HEAD_DIM)
            vp = jnp.take(cache_v[li], page_table, axis=0).reshape(n_seq, past, SB_HEADS, SB_HEAD_DIM)
            return sb_sample(q, k, v, kp, vp, bias)
        return fn

    y_sample, (k_s, v_s, lh_s, lc_s, s_s, sc_s) = run_group(
        x_sample, p_sample, past, state_lru_h, state_lru_conv, state_ssd, state_ssd_conv, sample_sb, W)

    return (y_prompt, y_sample, k_p, v_p, lh_p, lc_p, s_p, sc_p, k_s, v_s, lh_s, lc_s, s_s, sc_s)
```

```python
import functools
import math

import jax
import jax.numpy as jnp
from jax import lax
from jax.experimental import pallas as pl
from jax.experimental.pallas import tpu as pltpu

F32 = jnp.float32
BF16 = jnp.bfloat16

D_MODEL = 1024
DEPTH = 2
PAGE_SIZE = 128
LRU_WIDTH = 256
LRU_BLOCKS = 4
LRU_C = 8.0
SB_HEADS = 8
SB_HEAD_DIM = 64
SB_WIDTH = SB_HEADS * SB_HEAD_DIM
SSD_HEADS = 4
SSD_HEAD_DIM = 64
SSD_D_INNER = SSD_HEADS * SSD_HEAD_DIM
SSD_GROUPS = 2
SSD_D_STATE = 64
SSD_CONV_DIM = SSD_D_INNER + 2 * SSD_GROUPS * SSD_D_STATE
SSD_CHUNK = 128
CONV_W = 4
N_EXPERTS = 8
PLE_DIM = 256
DN_ALPHA = (2 * DEPTH) ** 0.25
LN_EPS = 1e-5
RMS_EPS = 1e-6

LANES = 128
SUBLANES = 8
VMEM_LIMIT = 56 * 1024 * 1024

C_XL, C_GL, C_Q, C_K, C_V, C_XBC, C_Z, C_DT, C_END = 0, 256, 512, 1024, 1536, 2048, 2560, 2816, 3072
SAMPLE_ROWS = 8
SB_TILE = 256
PAGES_PER_STEP = 8


def _cp(*sem):
    return pltpu.CompilerParams(dimension_semantics=sem, vmem_limit_bytes=VMEM_LIMIT)


def _layer_norm(x, g, b):
    mu = jnp.mean(x, axis=-1, keepdims=True)
    xc = x - mu
    var = jnp.mean(xc * xc, axis=-1, keepdims=True)
    return xc * lax.rsqrt(var + LN_EPS) * g + b


def _softplus(x):
    return jnp.maximum(x, 0.0) + jnp.log(1.0 + jnp.exp(-jnp.abs(x)))


def _sigmoid(x):
    return 1.0 / (1.0 + jnp.exp(-x))


def _silu(x):
    return x * _sigmoid(x)


def _gelu_tanh(x):
    return 0.5 * x * (1.0 + jnp.tanh(math.sqrt(2.0 / math.pi) * (x + 0.044715 * (x * x * x))))


def _row_iota(shape):
    return lax.broadcasted_iota(jnp.int32, shape, 0)


def _col_iota(shape):
    return lax.broadcasted_iota(jnp.int32, shape, 1)


def _shift_rows_prev(u, prev8, k):
    r = pltpu.roll(u, k, 0)
    head = jnp.where(_row_iota(prev8.shape) < k, pltpu.roll(prev8, k, 0), r[:SUBLANES])
    if u.shape[0] == SUBLANES:
        return head
    return jnp.concatenate([head, r[SUBLANES:]], axis=0)


def _causal_conv(u, prev8, cw, cb):
    out = cb + cw[3:4] * u
    for k in range(1, CONV_W):
        out = out + cw[CONV_W - 1 - k:CONV_W - k] * _shift_rows_prev(u, prev8, k)
    return out


def _shift_rows_fill(x, d, fill):
    return jnp.where(_row_iota(x.shape) >= d, pltpu.roll(x, d, 0), fill)


def _ln_body(x_ref, g_ref, b_ref, o_ref):
    o_ref[...] = _layer_norm(x_ref[...], g_ref[...], b_ref[...])


def ln_rows(x, g, b, tm):
    t, d = x.shape
    row = pl.BlockSpec((tm, d), lambda i: (i, 0))
    vec = pl.BlockSpec((1, d), lambda i: (0, 0))
    return pl.pallas_call(
        _ln_body, grid=(t // tm,), in_specs=[row, vec, vec], out_specs=row,
        out_shape=jax.ShapeDtypeStruct((t, d), F32), compiler_params=_cp("parallel"),
        name="ln_in")(x, g.reshape(1, d), b.reshape(1, d))


def _mm_body(x_ref, w_ref, o_ref):
    o_ref[...] = jnp.dot(x_ref[...].astype(BF16), w_ref[...], preferred_element_type=F32)


def matmul(x, w, tm, tn):
    t, k = x.shape
    n = w.shape[1]
    return pl.pallas_call(
        _mm_body, grid=(n // tn, t // tm),
        in_specs=[pl.BlockSpec((tm, k), lambda j, i: (i, 0)), pl.BlockSpec((k, tn), lambda j, i: (0, j))],
        out_specs=pl.BlockSpec((tm, tn), lambda j, i: (i, j)),
        out_shape=jax.ShapeDtypeStruct((t, n), F32), compiler_params=_cp("parallel", "parallel"),
        name="in_proj")(x, w)


def _outproj_body(yl_ref, ys_ref, yd_ref, x_ref, w_ref, g_ref, b_ref, o_ref):
    mixed = jnp.concatenate([yl_ref[...], ys_ref[...], yd_ref[...]], axis=-1).astype(BF16)
    mix = jnp.dot(mixed, w_ref[...], preferred_element_type=F32)
    o_ref[...] = _layer_norm(DN_ALPHA * x_ref[...] + mix, g_ref[...], b_ref[...])


def outproj_ln1(y_lru, y_sb, y_ssd, x, w, g, b, tm):
    t, d = x.shape
    rows = lambda c: pl.BlockSpec((tm, c), lambda i: (i, 0))
    vec = pl.BlockSpec((1, d), lambda i: (0, 0))
    return pl.pallas_call(
        _outproj_body, grid=(t // tm,),
        in_specs=[rows(LRU_WIDTH), rows(SB_WIDTH), rows(SSD_D_INNER), rows(d),
                  pl.BlockSpec((d, d), lambda i: (0, 0)), vec, vec],
        out_specs=rows(d), out_shape=jax.ShapeDtypeStruct((t, d), F32),
        compiler_params=_cp("parallel"), name="outproj_ln1")(y_lru, y_sb, y_ssd, x, w, g.reshape(1, d), b.reshape(1, d))


def _ffn_body(x_ref, wg_ref, wu_ref, wd_ref, o_ref, acc_s):
    f = pl.program_id(1)

    @pl.when(f == 0)
    def _():
        acc_s[...] = jnp.zeros_like(acc_s)

    xb = x_ref[...].astype(BF16)
    g = jnp.dot(xb, wg_ref[...], preferred_element_type=F32)
    u = jnp.dot(xb, wu_ref[...], preferred_element_type=F32)
    h = (_silu(g) * u).astype(BF16)
    acc_s[...] += jnp.dot(h, wd_ref[...], preferred_element_type=F32)

    @pl.when(f == pl.num_programs(1) - 1)
    def _():
        o_ref[...] = acc_s[...]


def ffn_dense(x, wg, wu, wd, tm, tf):
    t, d = x.shape
    ff = wg.shape[1]
    return pl.pallas_call(
        _ffn_body, grid=(t // tm, ff // tf),
        in_specs=[pl.BlockSpec((tm, d), lambda i, f: (i, 0)),
                  pl.BlockSpec((d, tf), lambda i, f: (0, f)),
                  pl.BlockSpec((d, tf), lambda i, f: (0, f)),
                  pl.BlockSpec((tf, d), lambda i, f: (f, 0))],
        out_specs=pl.BlockSpec((tm, d), lambda i, f: (i, 0)),
        out_shape=jax.ShapeDtypeStruct((t, d), F32),
        scratch_shapes=[pltpu.VMEM((tm, d), F32)],
        compiler_params=_cp("parallel", "arbitrary"), name="ffn_dense")(x, wg, wu, wd)


def _router_body(x_ref, w_ref, o_ref):
    logits = jnp.dot(x_ref[...], w_ref[...], preferred_element_type=F32, precision=lax.Precision.HIGHEST)
    lane = _col_iota(logits.shape)
    logits = jnp.where(lane < N_EXPERTS, logits, -jnp.inf)
    m1 = jnp.max(logits, axis=-1, keepdims=True)
    i1 = jnp.min(jnp.where(logits == m1, lane, LANES), axis=-1, keepdims=True)
    rest = jnp.where(lane == i1, -jnp.inf, logits)
    m2 = jnp.max(rest, axis=-1, keepdims=True)
    i2 = jnp.min(jnp.where(rest == m2, lane, LANES), axis=-1, keepdims=True)
    e2 = jnp.exp(m2 - m1)
    w1 = 1.0 / (1.0 + e2)
    o_ref[...] = jnp.where(lane == i1, w1, 0.0) + jnp.where(lane == i2, e2 * w1, 0.0)


def router_gates(x, w_router, tm):
    t, d = x.shape
    wpad = jnp.pad(w_router, ((0, 0), (0, LANES - N_EXPERTS)))
    return pl.pallas_call(
        _router_body, grid=(t // tm,),
        in_specs=[pl.BlockSpec((tm, d), lambda i: (i, 0)), pl.BlockSpec((d, LANES), lambda i: (0, 0))],
        out_specs=pl.BlockSpec((tm, LANES), lambda i: (i, 0)),
        out_shape=jax.ShapeDtypeStruct((t, LANES), F32), compiler_params=_cp("parallel"), name="router")(x, wpad)


def _moe_body(x_ref, gates_ref, wg_ref, wu_ref, wd_ref, o_ref, acc_s):
    e = pl.program_id(1)
    f = pl.program_id(2)

    @pl.when(jnp.logical_and(e == 0, f == 0))
    def _():
        acc_s[...] = jnp.zeros_like(acc_s)

    xb = x_ref[...].astype(BF16)
    g = jnp.dot(xb, wg_ref[...], preferred_element_type=F32)
    u = jnp.dot(xb, wu_ref[...], preferred_element_type=F32)
    gates = gates_ref[...]
    gate = jnp.sum(jnp.where(_col_iota(gates.shape) == e, gates, 0.0), axis=-1, keepdims=True)
    h = (_silu(g) * u * gate).astype(BF16)
    acc_s[...] += jnp.dot(h, wd_ref[...], preferred_element_type=F32)

    @pl.when(jnp.logical_and(e == pl.num_programs(1) - 1, f == pl.num_programs(2) - 1))
    def _():
        o_ref[...] = acc_s[...]


def moe_dense(x, gates, wg, wu, wd, tm, tf):
    t, d = x.shape
    ne, _, ff = wg.shape
    return pl.pallas_call(
        _moe_body, grid=(t // tm, ne, ff // tf),
        in_specs=[pl.BlockSpec((tm, d), lambda i, e, f: (i, 0)),
                  pl.BlockSpec((tm, LANES), lambda i, e, f: (i, 0)),
                  pl.BlockSpec((None, d, tf), lambda i, e, f: (e, 0, f)),
                  pl.BlockSpec((None, d, tf), lambda i, e, f: (e, 0, f)),
                  pl.BlockSpec((None, tf, d), lambda i, e, f: (e, f, 0))],
        out_specs=pl.BlockSpec((tm, d), lambda i, e, f: (i, 0)),
        out_shape=jax.ShapeDtypeStruct((t, d), F32),
        scratch_shapes=[pltpu.VMEM((tm, d), F32)],
        compiler_params=_cp("parallel", "arbitrary", "arbitrary"), name="moe_dense")(x, gates, wg, wu, wd)


def _ple_ln2_body(x_ref, f_ref, p_ref, wg_ref, wp_ref, g_ref, b_ref, o_ref):
    x1 = x_ref[...]
    gate = _sigmoid(jnp.dot(x1.astype(BF16), wg_ref[...], preferred_element_type=F32))
    proj = jnp.dot(p_ref[...].astype(BF16), wp_ref[...], preferred_element_type=F32)
    o_ref[...] = _layer_norm(DN_ALPHA * x1 + f_ref[...] + gate * proj, g_ref[...], b_ref[...])


def ple_ln2(x1, f, p, wg, wp, g, b, tm):
    t, d = x1.shape
    rows = lambda c: pl.BlockSpec((tm, c), lambda i: (i, 0))
    vec = pl.BlockSpec((1, d), lambda i: (0, 0))
    return pl.pallas_call(
        _ple_ln2_body, grid=(t // tm,),
        in_specs=[rows(d), rows(d), rows(PLE_DIM), pl.BlockSpec((d, d), lambda i: (0, 0)),
                  pl.BlockSpec((PLE_DIM, d), lambda i: (0, 0)), vec, vec],
        out_specs=rows(d), out_shape=jax.ShapeDtypeStruct((t, d), F32),
        compiler_params=_cp("parallel"), name="ple_ln2")(x1, f, p, wg, wp, g.reshape(1, d), b.reshape(1, d))


def _lru_body(x_ref, g_ref, prev0_ref, h0_ref, cw_ref, cb_ref, wg_ref, bg_ref, lam_ref,
              y_ref, ht_ref, prev_s, h_s, *, tl, first_is_pos0, valid_len):
    i = pl.program_id(1)

    @pl.when(i == 0)
    def _():
        prev_s[...] = prev0_ref[...]
        h_s[...] = h0_ref[...]

    u = x_ref[...]
    xc = _causal_conv(u, prev_s[...], cw_ref[...], cb_ref[...])
    prev_s[...] = u[tl - SUBLANES:tl]

    gates = jnp.dot(xc.astype(BF16), wg_ref[...], preferred_element_type=F32) + bg_ref[...]
    r = _sigmoid(gates[:, :LRU_WIDTH])
    ig = _sigmoid(gates[:, LRU_WIDTH:])
    a = jnp.exp(-LRU_C * r * _softplus(-lam_ref[...]))
    mult = jnp.sqrt(1.0 - a * a)
    row = i * tl + _row_iota(u.shape)
    if first_is_pos0:
        mult = jnp.where(row == 0, 1.0, mult)
    b = mult * ig * xc
    if valid_len is not None:
        valid = row < valid_len
        a = jnp.where(valid, a, 1.0)
        b = jnp.where(valid, b, 0.0)

    d = 1
    while d < tl:
        b = b + a * _shift_rows_fill(b, d, 0.0)
        a = a * _shift_rows_fill(a, d, 1.0)
        d *= 2
    hs = b + a * h_s[...]
    h_last = hs[tl - 1:tl]
    h_s[...] = h_last
    ht_ref[...] = h_last
    y_ref[...] = hs * _gelu_tanh(g_ref[...])


def lru_mixer(proj, nb, seq, tl, prev0, h0, cw, cb, wgate, bgate, lam, first_is_pos0, valid_len):
    nl = seq // tl
    w = LRU_WIDTH
    vec = lambda r, c: pl.BlockSpec((r, c), lambda b, i: (0, 0))
    body = functools.partial(_lru_body, tl=tl, first_is_pos0=first_is_pos0, valid_len=valid_len)
    return pl.pallas_call(
        body, grid=(nb, nl),
        in_specs=[pl.BlockSpec((tl, w), lambda b, i: (b * nl + i, C_XL // w)),
                  pl.BlockSpec((tl, w), lambda b, i: (b * nl + i, C_GL // w)),
                  pl.BlockSpec((None, SUBLANES, w), lambda b, i: (b, 0, 0)),
                  pl.BlockSpec((None, 1, w), lambda b, i: (b, 0, 0)),
                  vec(SUBLANES, w), vec(1, w), vec(w, 2 * w), vec(1, 2 * w), vec(1, w)],
        out_specs=[pl.BlockSpec((tl, w), lambda b, i: (b * nl + i, 0)),
                   pl.BlockSpec((None, 1, w), lambda b, i: (b, 0, 0))],
        out_shape=[jax.ShapeDtypeStruct((nb * seq, w), F32), jax.ShapeDtypeStruct((nb, 1, w), F32)],
        scratch_shapes=[pltpu.VMEM((SUBLANES, w), F32), pltpu.VMEM((1, w), F32)],
        compiler_params=_cp("parallel", "arbitrary"), name="lru")(
            proj, proj, prev0, h0, cw, cb, wgate, bgate, lam)


def _ssd_body(xbc_ref, z_ref, dt_ref, prev0_ref, s0_ref, cw_ref, cb_ref, dtb_ref, a_ref, d_ref, nw_ref,
              y_ref, st_ref, prev_s, s_s, *, rows, valid_len):
    i = pl.program_id(1)
    q = SSD_CHUNK
    hp = SSD_HEAD_DIM
    ns = SSD_D_STATE

    @pl.when(i == 0)
    def _():
        prev_s[...] = prev0_ref[...]
        s_s[...] = s0_ref[...]

    u = xbc_ref[...]
    xbc = _silu(_causal_conv(u, prev_s[...], cw_ref[...], cb_ref[...]))
    prev_s[...] = u[rows - SUBLANES:rows]
    dt = _softplus(dt_ref[...] + dtb_ref[...])
    if valid_len is not None:
        dt = jnp.where(i * rows + _row_iota(dt.shape) < valid_len, dt, 0.0)
    if rows < q:
        xbc = jnp.concatenate([xbc, jnp.zeros((q - rows, xbc.shape[1]), F32)], axis=0)
        dt = jnp.concatenate([dt, jnp.zeros((q - rows, dt.shape[1]), F32)], axis=0)
    xs = xbc[:, :SSD_D_INNER]
    bm = xbc[:, SSD_D_INNER:SSD_D_INNER + SSD_GROUPS * ns]
    cm = xbc[:, SSD_D_INNER + SSD_GROUPS * ns:]

    a_cs = dt * a_ref[...]
    d = 1
    while d < q:
        a_cs = a_cs + _shift_rows_fill(a_cs, d, 0.0)
        d *= 2
    a_cs_t = a_cs.T
    tri = _row_iota((q, q)) >= _col_iota((q, q))

    rep = SSD_HEADS // SSD_GROUPS
    nt = (((1,), (1,)), ((), ()))
    tn = (((0,), (0,)), ((), ()))
    cb_g = [lax.dot_general(cm[:, g * ns:(g + 1) * ns], bm[:, g * ns:(g + 1) * ns], nt,
                            preferred_element_type=F32) for g in range(SSD_GROUPS)]
    ys = []
    for h in range(SSD_HEADS):
        g = h // rep
        col = a_cs[:, h:h + 1]
        seg = col - a_cs_t[h:h + 1, :]
        lmat = jnp.exp(jnp.where(tri, seg, -jnp.inf))
        xs_h = xs[:, h * hp:(h + 1) * hp]
        xd = xs_h * dt[:, h:h + 1]
        c_g = cm[:, g * ns:(g + 1) * ns]
        b_g = bm[:, g * ns:(g + 1) * ns]
        s_prev = s_s[h]
        y_diag = jnp.dot(cb_g[g] * lmat, xd, preferred_element_type=F32)
        y_off = jnp.exp(col) * lax.dot_general(c_g, s_prev, nt, preferred_element_type=F32)
        a_last = a_cs[q - 1:q, h:h + 1]
        s_new = jnp.exp(a_last) * s_prev + lax.dot_general(xd * jnp.exp(a_last - col), b_g, tn,
                                                            preferred_element_type=F32)
        s_s[h] = s_new
        st_ref[h] = s_new
        ys.append(y_diag + y_off)
    y = jnp.concatenate(ys, axis=-1) + d_ref[...] * xs
    if rows < q:
        y = y[:rows]
    y = y * _silu(z_ref[...])
    y_ref[...] = y * lax.rsqrt(jnp.mean(y * y, axis=-1, keepdims=True) + RMS_EPS) * nw_ref[...]


def ssd_mixer(proj, nb, seq, rows, prev0, s0, cw, cb, dtb, a_neg, d_skip, norm_w, valid_len):
    nl = seq // rows
    cdim = SSD_CONV_DIM
    di = SSD_D_INNER
    vec = lambda r, c: pl.BlockSpec((r, c), lambda b, i: (0, 0))
    st = pl.BlockSpec((None, SSD_HEADS, SSD_HEAD_DIM, SSD_D_STATE), lambda b, i: (b, 0, 0, 0))
    body = functools.partial(_ssd_body, rows=rows, valid_len=valid_len)
    return pl.pallas_call(
        body, grid=(nb, nl),
        in_specs=[pl.BlockSpec((rows, cdim), lambda b, i: (b * nl + i, C_XBC // cdim)),
                  pl.BlockSpec((rows, di), lambda b, i: (b * nl + i, C_Z // di)),
                  pl.BlockSpec((rows, LANES), lambda b, i: (b * nl + i, C_DT // LANES)),
                  pl.BlockSpec((None, SUBLANES, cdim), lambda b, i: (b, 0, 0)),
                  st, vec(SUBLANES, cdim), vec(1, cdim), vec(1, LANES), vec(1, LANES), vec(1, di), vec(1, di)],
        out_specs=[pl.BlockSpec((rows, di), lambda b, i: (b * nl + i, 0)), st],
        out_shape=[jax.ShapeDtypeStruct((nb * seq, di), F32),
                   jax.ShapeDtypeStruct((nb, SSD_HEADS, SSD_HEAD_DIM, SSD_D_STATE), F32)],
        scratch_shapes=[pltpu.VMEM((SUBLANES, cdim), F32),
                        pltpu.VMEM((SSD_HEADS, SSD_HEAD_DIM, SSD_D_STATE), F32)],
        compiler_params=_cp("parallel", "arbitrary"), name="ssd")(
            proj, proj, proj, prev0, s0, cw, cb, dtb, a_neg, d_skip, norm_w)


def _sb_block(qh, kblk, vblk, bias, neg_upper, mask):
    nt = (((1,), (1,)), ((), ()))
    z = lax.dot_general(qh, kblk, nt, preferred_element_type=F32) + bias
    sp = _softplus(z)
    log_beta = z - sp
    if mask is not None:
        sp = jnp.where(mask, sp, 0.0)
    hi = sp.astype(BF16)
    lo = (sp - hi.astype(F32)).astype(BF16)
    later = (jnp.dot(hi, neg_upper, preferred_element_type=F32)
             + jnp.dot(lo, neg_upper, preferred_element_type=F32))
    p = jnp.exp(log_beta + later)
    if mask is not None:
        p = jnp.where(mask, p, 0.0)
    pv = jnp.dot(p.astype(BF16), vblk, preferred_element_type=F32)
    return pv, jnp.sum(sp, axis=-1, keepdims=True)


def _neg_strict_upper(n):
    return jnp.where(_row_iota((n, n)) > _col_iota((n, n)), -1.0, 0.0).astype(BF16)


def _sb_prompt_body(bias_ref, q_ref, k_ref, v_ref, o_ref, *, tq):
    i = pl.program_id(1)
    hd = SB_HEAD_DIM
    neg_upper = _neg_strict_upper(tq)
    diag_mask = _col_iota((tq, tq)) < _row_iota((tq, tq))
    scale = 1.0 / math.sqrt(hd)
    for h in range(SB_HEADS):
        cols = slice(h * hd, (h + 1) * hd)
        qh = q_ref[:, cols] * scale
        bias = bias_ref[h]

        def block(kb, mask):
            r0 = pl.multiple_of(kb * tq, tq)
            return _sb_block(qh, k_ref[pl.ds(r0, tq), cols], v_ref[pl.ds(r0, tq), cols], bias, neg_upper, mask)

        acc, c = block(i, diag_mask)

        def body(kk, carry):
            c, acc = carry
            pv, rs = block(i - kk, None)
            return c + rs, acc + jnp.exp(-c) * pv

        c, acc = lax.fori_loop(1, i + 1, body, (c, acc))
        o_ref[:, cols] = acc


def sb_prompt(q, k, v, bias, nb, seq):
    tq = SB_TILE
    nq = seq // tq
    w = SB_WIDTH
    kv = pl.BlockSpec((seq, w), lambda b, i: (b, 0))
    return pl.pallas_call(
        functools.partial(_sb_prompt_body, tq=tq), grid=(nb, nq),
        in_specs=[pl.BlockSpec(memory_space=pltpu.SMEM),
                  pl.BlockSpec((tq, w), lambda b, i: (b * nq + i, 0)), kv, kv],
        out_specs=pl.BlockSpec((tq, w), lambda b, i: (b * nq + i, 0)),
        out_shape=jax.ShapeDtypeStruct((nb * seq, w), F32),
        compiler_params=_cp("parallel", "arbitrary"), name="sb_prompt")(bias, q, k, v)


def _sb_sample_body(pt_ref, qbd_ref, bias_ref, kn_ref, vn_ref, *refs, npg, t_new):
    del pt_ref
    k_refs = refs[:npg]
    v_refs = refs[npg:2 * npg]
    o_ref, c_s, acc_s = refs[2 * npg:]
    j = pl.program_id(1)
    ps = PAGE_SIZE
    qbd = qbd_ref[...]
    bias = bias_ref[...]
    neg_upper = _neg_strict_upper(ps)
    nrow = qbd.shape[0]

    @pl.when(j == 0)
    def _():
        pad = jnp.zeros((ps - kn_ref.shape[0], kn_ref.shape[1]), F32)
        kn = jnp.concatenate([kn_ref[...], pad], axis=0).astype(BF16)
        vn = jnp.concatenate([vn_ref[...], pad], axis=0).astype(BF16)
        mask = _col_iota((nrow, ps)) < lax.rem(_row_iota((nrow, ps)), t_new)
        pv, rs = _sb_block(qbd, kn, vn, bias, neg_upper, mask)
        acc_s[...] = pv
        c_s[...] = rs

    c = c_s[...]
    acc = acc_s[...]
    for n in range(npg):
        pv, rs = _sb_block(qbd, k_refs[n][...].astype(BF16), v_refs[n][...].astype(BF16), bias, neg_upper, None)
        acc = acc + jnp.exp(-c) * pv
        c = c + rs
    c_s[...] = c
    acc_s[...] = acc

    @pl.when(j == pl.num_programs(1) - 1)
    def _():
        o_ref[...] = acc


def sb_sample(qbd, bias_rows, k_new, v_new, cache_k, cache_v, page_table, layer, t_new):
    nseq, npages = page_table.shape
    npg = PAGES_PER_STEP
    nsteps = npages // npg
    w = SB_WIDTH
    nrow = qbd.shape[1]
    ck = cache_k.reshape(cache_k.shape[0], cache_k.shape[1], PAGE_SIZE, w)
    cv = cache_v.reshape(cache_v.shape[0], cache_v.shape[1], PAGE_SIZE, w)

    def page_spec(n):
        return pl.BlockSpec((None, None, PAGE_SIZE, w),
                            lambda b, j, pt: (layer, pt[b, npages - 1 - (j * npg + n)], 0, 0))

    per_seq = lambda r: pl.BlockSpec((None, r, w), lambda b, j, pt: (b, 0, 0))
    grid_spec = pltpu.PrefetchScalarGridSpec(
        num_scalar_prefetch=1, grid=(nseq, nsteps),
        in_specs=[per_seq(nrow), pl.BlockSpec((nrow, 1), lambda b, j, pt: (0, 0)),
                  per_seq(SAMPLE_ROWS), per_seq(SAMPLE_ROWS)]
                 + [page_spec(n) for n in range(npg)] * 2,
        out_specs=per_seq(nrow),
        scratch_shapes=[pltpu.VMEM((nrow, 1), F32), pltpu.VMEM((nrow, w), F32)])
    return pl.pallas_call(
        functools.partial(_sb_sample_body, npg=npg, t_new=t_new), grid_spec=grid_spec,
        out_shape=jax.ShapeDtypeStruct((nseq, nrow, w), F32),
        compiler_params=_cp("parallel", "arbitrary"), name="sb_sample")(
            page_table, qbd, bias_rows, k_new, v_new, *([ck] * npg), *([cv] * npg))


def _pad_prev(buf):
    return jnp.pad(buf, ((0, 0), (SUBLANES - (CONV_W - 1), 0), (0, 0)))


def _prep_weights(W):
    P = {}
    w_in = W['w_in']
    o = [0, 256, 512, 1024, 1536, 2048, 2304, 2816, 3328]
    P['w_in'] = jnp.concatenate(
        [w_in[:, :, o[0]:o[5]], w_in[:, :, o[6]:o[7]], w_in[:, :, o[5]:o[6]],
         jnp.pad(w_in[:, :, o[7]:o[8]], ((0, 0), (0, 0), (0, C_END - C_DT - SSD_HEADS)))], axis=-1).astype(BF16)
    eye = jnp.eye(LRU_BLOCKS, dtype=F32)
    bd = lambda w: jnp.einsum('lgij,gh->lgihj', w, eye).reshape(DEPTH, LRU_WIDTH, LRU_WIDTH)
    P['lru_wgate'] = jnp.concatenate([bd(W['lru_wa']), bd(W['lru_wx'])], axis=-1).astype(BF16)
    P['lru_bgate'] = jnp.concatenate([W['lru_ba'], W['lru_bx']], axis=-1)[:, None, :]
    pad_cw = lambda w: jnp.pad(w, ((0, 0), (0, SUBLANES - CONV_W), (0, 0)))
    P['lru_cw'] = pad_cw(W['lru_conv_w'])
    P['ssd_cw'] = pad_cw(W['ssd_conv_w'])
    lane_pad = lambda v: jnp.pad(v, ((0, 0), (0, LANES - SSD_HEADS)))[:, None, :]
    P['ssd_dtb'] = lane_pad(W['ssd_dt_bias'])
    P['ssd_a'] = lane_pad(-jnp.exp(W['ssd_a_log']))
    P['ssd_d'] = jnp.repeat(W['ssd_d'], SSD_HEAD_DIM, axis=-1)[:, None, :]
    for name in ('w_out', 'ffn_w_gate', 'ffn_w_up', 'ffn_w_down', 'moe_w_gate', 'moe_w_up', 'moe_w_down',
                 'ple_w_gate', 'ple_w_proj'):
        P[name] = W[name].astype(BF16)
    return P


def _run_group(x, p, nb, seq, valid_len, at_pos0, lru_h0, lru_buf0, ssd_s0, ssd_buf0, sb_fn, W, P):
    t = nb * seq
    tm = min(512, t)
    tl = min(256, seq)
    rows = min(SSD_CHUNK, seq)
    vl = None if valid_len == seq else valid_len
    x = ln_rows(x, W['ln_in_g'], W['ln_in_b'], tm)
    states = []
    for li in range(DEPTH):
        proj = matmul(x, P['w_in'][li], tm, C_END // 2)
        y_lru, lru_ht = lru_mixer(proj, nb, seq, tl, _pad_prev(lru_buf0[li]), lru_h0[li][:, None, :],
                                  P['lru_cw'][li], W['lru_conv_b'][li][None], P['lru_wgate'][li],
                                  P['lru_bgate'][li], W['lru_lambda'][li][None], at_pos0, vl)
        y_ssd, ssd_st = ssd_mixer(proj, nb, seq, rows, _pad_prev(ssd_buf0[li]), ssd_s0[li],
                                  P['ssd_cw'][li], W['ssd_conv_b'][li][None], P['ssd_dtb'][li], P['ssd_a'][li],
                                  P['ssd_d'][li], W['ssd_norm_w'][li][None], vl)
        y_sb = sb_fn(li, proj)
        x1 = outproj_ln1(y_lru, y_sb, y_ssd, x, P['w_out'][li], W['ln1_g'][li], W['ln1_b'][li], tm)
        j = li // 2
        if li % 2 == 0:
            f = ffn_dense(x1, P['ffn_w_gate'][j], P['ffn_w_up'][j], P['ffn_w_down'][j], tm, 1408)
        else:
            gates = router_gates(x1, W['moe_router'][j], tm)
            f = moe_dense(x1, gates, P['moe_w_gate'][j], P['moe_w_up'][j], P['moe_w_down'][j], tm, 1792)
        x = ple_ln2(x1, f, p[li], P['ple_w_gate'][li], P['ple_w_proj'][li], W['ln2_g'][li], W['ln2_b'][li], tm)
        p3 = proj.reshape(nb, seq, C_END)
        tail = slice(valid_len - (CONV_W - 1), valid_len)
        states.append((p3[:, :valid_len, C_K:C_K + SB_WIDTH].reshape(nb, valid_len, SB_HEADS, SB_HEAD_DIM),
                       p3[:, :valid_len, C_V:C_V + SB_WIDTH].reshape(nb, valid_len, SB_HEADS, SB_HEAD_DIM),
                       lru_ht[:, 0], p3[:, tail, C_XL:C_XL + LRU_WIDTH], ssd_st,
                       p3[:, tail, C_XBC:C_XBC + SSD_CONV_DIM]))
    stacked = tuple(jnp.stack([st[n] for st in states]) for n in range(6))
    return x, stacked


def kernel(x_prompt, x_sample, p_prompt, p_sample, cache_k, cache_v, page_table, state_lru_h, state_lru_conv, state_ssd, state_ssd_conv, ln_in_g, ln_in_b, w_in, lru_conv_w, lru_conv_b, lru_wa, lru_ba, lru_wx, lru_bx, lru_lambda, sb_bias, ssd_conv_w, ssd_conv_b, ssd_dt_bias, ssd_a_log, ssd_d, ssd_norm_w, w_out, ln1_g, ln1_b, ln2_g, ln2_b, ffn_w_gate, ffn_w_up, ffn_w_down, moe_router, moe_w_gate, moe_w_up, moe_w_down, ple_w_gate, ple_w_proj):
    W = dict(ln_in_g=ln_in_g, ln_in_b=ln_in_b, w_in=w_in, lru_conv_w=lru_conv_w, lru_conv_b=lru_conv_b,
             lru_wa=lru_wa, lru_ba=lru_ba, lru_wx=lru_wx, lru_bx=lru_bx, lru_lambda=lru_lambda,
             sb_bias=sb_bias, ssd_conv_w=ssd_conv_w, ssd_conv_b=ssd_conv_b, ssd_dt_bias=ssd_dt_bias,
             ssd_a_log=ssd_a_log, ssd_d=ssd_d, ssd_norm_w=ssd_norm_w, w_out=w_out, ln1_g=ln1_g,
             ln1_b=ln1_b, ln2_g=ln2_g, ln2_b=ln2_b, ffn_w_gate=ffn_w_gate, ffn_w_up=ffn_w_up,
             ffn_w_down=ffn_w_down, moe_router=moe_router, moe_w_gate=moe_w_gate, moe_w_up=moe_w_up,
             moe_w_down=moe_w_down, ple_w_gate=ple_w_gate, ple_w_proj=ple_w_proj)
    P = _prep_weights(W)
    d = D_MODEL

    bp, sp, _ = x_prompt.shape
    zeros = lambda *s: jnp.zeros((DEPTH, bp) + s, F32)

    def prompt_sb(li, proj):
        qkv = proj[:, C_Q:C_V + SB_WIDTH].astype(BF16)
        return sb_prompt(qkv[:, :SB_WIDTH], qkv[:, SB_WIDTH:2 * SB_WIDTH], qkv[:, 2 * SB_WIDTH:],
                         sb_bias[li], bp, sp)

    y_prompt, st_p = _run_group(
        x_prompt.reshape(bp * sp, d), p_prompt.reshape(DEPTH, bp * sp, PLE_DIM), bp, sp, sp, True,
        zeros(LRU_WIDTH), zeros(CONV_W - 1, LRU_WIDTH), zeros(SSD_HEADS, SSD_HEAD_DIM, SSD_D_STATE),
        zeros(CONV_W - 1, SSD_CONV_DIM), prompt_sb, W, P)

    bs, ts, _ = x_sample.shape
    rs = SAMPLE_ROWS
    pad_rows = lambda a, ax: jnp.pad(a, [(0, rs - ts) if n == ax else (0, 0) for n in range(a.ndim)])
    eye_h = jnp.eye(SB_HEADS, dtype=F32)
    scale = 1.0 / math.sqrt(SB_HEAD_DIM)

    def sample_sb(li, proj):
        p3 = proj.reshape(bs, rs, C_END)
        q = p3[:, :ts, C_Q:C_Q + SB_WIDTH].reshape(bs, ts, SB_HEADS, SB_HEAD_DIM) * scale
        qbd = jnp.einsum('bthd,hg->bhtgd', q, eye_h).reshape(bs, SB_HEADS * ts, SB_WIDTH).astype(BF16)
        bias_rows = jnp.repeat(sb_bias[li], ts)[:, None]
        out = sb_sample(qbd, bias_rows, p3[:, :, C_K:C_K + SB_WIDTH], p3[:, :, C_V:C_V + SB_WIDTH],
                        cache_k, cache_v, page_table, li, ts)
        y = jnp.einsum('bhtgd,hg->bthd', out.reshape(bs, SB_HEADS, ts, SB_HEADS, SB_HEAD_DIM), eye_h)
        return pad_rows(y.reshape(bs, ts, SB_WIDTH), 1).reshape(bs * rs, SB_WIDTH)

    y_sample, st_s = _run_group(
        pad_rows(x_sample, 1).reshape(bs * rs, d), pad_rows(p_sample, 2).reshape(DEPTH, bs * rs, PLE_DIM),
        bs, rs, ts, False, state_lru_h, state_lru_conv, state_ssd, state_ssd_conv, sample_sb, W, P)
    y_sample = y_sample.reshape(bs, rs, d)[:, :ts]

    return (y_prompt.reshape(bp, sp, d), y_sample) + st_p + st_s
```

```python
import functools
import math

import jax
import jax.numpy as jnp
from jax import lax
from jax.experimental import pallas as pl
from jax.experimental.pallas import tpu as pltpu

F32 = jnp.float32
BF16 = jnp.bfloat16

D_MODEL = 1024
DEPTH = 2
PAGE_SIZE = 128
LRU_WIDTH = 256
LRU_BLOCKS = 4
LRU_C = 8.0
SB_HEADS = 8
SB_HEAD_DIM = 64
SB_WIDTH = SB_HEADS * SB_HEAD_DIM
SSD_HEADS = 4
SSD_HEAD_DIM = 64
SSD_D_INNER = SSD_HEADS * SSD_HEAD_DIM
SSD_GROUPS = 2
SSD_D_STATE = 64
SSD_CONV_DIM = SSD_D_INNER + 2 * SSD_GROUPS * SSD_D_STATE
SSD_CHUNK = 128
CONV_W = 4
N_EXPERTS = 8
PLE_DIM = 256
DN_ALPHA = (2 * DEPTH) ** 0.25
LN_EPS = 1e-5
RMS_EPS = 1e-6
LOG2E = math.log2(math.e)

LANES = 128
SUBLANES = 8
VMEM_LIMIT = 56 * 1024 * 1024

C_XL, C_GL, C_Q, C_K, C_V, C_XBC, C_Z, C_DT, C_END = 0, 256, 512, 1024, 1536, 2048, 2560, 2816, 3072
SAMPLE_ROWS = 8
SB_TILE = 256
PAGES_PER_STEP = 16
MOE_SPARSE_MIN_TOKENS = 4096


def _cp(*sem):
    return pltpu.CompilerParams(dimension_semantics=sem, vmem_limit_bytes=VMEM_LIMIT)


def _layer_norm(x, g, b):
    mu = jnp.mean(x, axis=-1, keepdims=True)
    xc = x - mu
    var = jnp.mean(xc * xc, axis=-1, keepdims=True)
    return xc * lax.rsqrt(var + LN_EPS) * g + b


def _softplus(x):
    return jnp.maximum(x, 0.0) + jnp.log(1.0 + jnp.exp(-jnp.abs(x)))


def _sigmoid(x):
    return 1.0 / (1.0 + jnp.exp(-x))


def _silu(x):
    return x * _sigmoid(x)


def _gelu_tanh(x):
    return 0.5 * x * (1.0 + jnp.tanh(math.sqrt(2.0 / math.pi) * (x + 0.044715 * (x * x * x))))


def _row_iota(shape):
    return lax.broadcasted_iota(jnp.int32, shape, 0)


def _col_iota(shape):
    return lax.broadcasted_iota(jnp.int32, shape, 1)


def _shift_rows_prev(u, prev8, k):
    r = pltpu.roll(u, k, 0)
    head = jnp.where(_row_iota(prev8.shape) < k, pltpu.roll(prev8, k, 0), r[:SUBLANES])
    if u.shape[0] == SUBLANES:
        return head
    return jnp.concatenate([head, r[SUBLANES:]], axis=0)


def _causal_conv(u, prev8, cw, cb):
    out = cb + cw[3:4] * u
    for k in range(1, CONV_W):
        out = out + cw[CONV_W - 1 - k:CONV_W - k] * _shift_rows_prev(u, prev8, k)
    return out


def _shift_rows_fill(x, d, fill):
    return jnp.where(_row_iota(x.shape) >= d, pltpu.roll(x, d, 0), fill)


def _ln_body(x_ref, g_ref, b_ref, o_ref):
    o_ref[...] = _layer_norm(x_ref[...], g_ref[...], b_ref[...])


def ln_rows(x, g, b, tm):
    t, d = x.shape
    row = pl.BlockSpec((tm, d), lambda i: (i, 0))
    vec = pl.BlockSpec((1, d), lambda i: (0, 0))
    return pl.pallas_call(
        _ln_body, grid=(t // tm,), in_specs=[row, vec, vec], out_specs=row,
        out_shape=jax.ShapeDtypeStruct((t, d), F32), compiler_params=_cp("parallel"),
        name="ln_in")(x, g.reshape(1, d), b.reshape(1, d))


def _inproj_body(x_ref, w_ref, o_ref, qkv_ref):
    y = jnp.dot(x_ref[...].astype(BF16), w_ref[...], preferred_element_type=F32)
    o_ref[...] = y
    qkv_ref[:, :SB_WIDTH] = (y[:, C_Q:C_Q + SB_WIDTH] * (LOG2E / math.sqrt(SB_HEAD_DIM))).astype(BF16)
    qkv_ref[:, SB_WIDTH:] = y[:, C_K:C_V + SB_WIDTH].astype(BF16)


def in_proj(x, w, tm):
    t, k = x.shape
    n = w.shape[1]
    return pl.pallas_call(
        _inproj_body, grid=(t // tm,),
        in_specs=[pl.BlockSpec((tm, k), lambda i: (i, 0)), pl.BlockSpec((k, n), lambda i: (0, 0))],
        out_specs=[pl.BlockSpec((tm, n), lambda i: (i, 0)), pl.BlockSpec((tm, 3 * SB_WIDTH), lambda i: (i, 0))],
        out_shape=[jax.ShapeDtypeStruct((t, n), F32), jax.ShapeDtypeStruct((t, 3 * SB_WIDTH), BF16)],
        compiler_params=_cp("parallel"), name="in_proj")(x, w)


def _outproj_body(yl_ref, ys_ref, yd_ref, x_ref, w_ref, g_ref, b_ref, o_ref):
    mixed = jnp.concatenate([yl_ref[...], ys_ref[...], yd_ref[...]], axis=-1).astype(BF16)
    mix = jnp.dot(mixed, w_ref[...], preferred_element_type=F32)
    o_ref[...] = _layer_norm(DN_ALPHA * x_ref[...] + mix, g_ref[...], b_ref[...])


def outproj_ln1(y_lru, y_sb, y_ssd, x, w, g, b, tm):
    t, d = x.shape
    rows = lambda c: pl.BlockSpec((tm, c), lambda i: (i, 0))
    vec = pl.BlockSpec((1, d), lambda i: (0, 0))
    return pl.pallas_call(
        _outproj_body, grid=(t // tm,),
        in_specs=[rows(LRU_WIDTH), rows(SB_WIDTH), rows(SSD_D_INNER), rows(d),
                  pl.BlockSpec((d, d), lambda i: (0, 0)), vec, vec],
        out_specs=rows(d), out_shape=jax.ShapeDtypeStruct((t, d), F32),
        compiler_params=_cp("parallel"), name="outproj_ln1")(y_lru, y_sb, y_ssd, x, w, g.reshape(1, d), b.reshape(1, d))


def _ffn_body(x_ref, wg_ref, wu_ref, wd_ref, o_ref, acc_s):
    f = pl.program_id(1)

    @pl.when(f == 0)
    def _():
        acc_s[...] = jnp.zeros_like(acc_s)

    xb = x_ref[...].astype(BF16)
    g = jnp.dot(xb, wg_ref[...], preferred_element_type=F32)
    u = jnp.dot(xb, wu_ref[...], preferred_element_type=F32)
    h = (_silu(g) * u).astype(BF16)
    acc_s[...] += jnp.dot(h, wd_ref[...], preferred_element_type=F32)

    @pl.when(f == pl.num_programs(1) - 1)
    def _():
        o_ref[...] = acc_s[...]


def ffn_dense(x, wg, wu, wd, tm, tf):
    t, d = x.shape
    ff = wg.shape[1]
    return pl.pallas_call(
        _ffn_body, grid=(t // tm, ff // tf),
        in_specs=[pl.BlockSpec((tm, d), lambda i, f: (i, 0)),
                  pl.BlockSpec((d, tf), lambda i, f: (0, f)),
                  pl.BlockSpec((d, tf), lambda i, f: (0, f)),
                  pl.BlockSpec((tf, d), lambda i, f: (f, 0))],
        out_specs=pl.BlockSpec((tm, d), lambda i, f: (i, 0)),
        out_shape=jax.ShapeDtypeStruct((t, d), F32),
        scratch_shapes=[pltpu.VMEM((tm, d), F32)],
        compiler_params=_cp("parallel", "arbitrary"), name="ffn_dense")(x, wg, wu, wd)


def _router_body(x_ref, w_ref, o_ref, sel_ref):
    logits = jnp.dot(x_ref[...], w_ref[...], preferred_element_type=F32, precision=lax.Precision.HIGHEST)
    lane = _col_iota(logits.shape).astype(F32)
    logits = jnp.where(lane < N_EXPERTS, logits, -jnp.inf)
    m1 = jnp.max(logits, axis=-1, keepdims=True)
    i1 = jnp.min(jnp.where(logits == m1, lane, float(LANES)), axis=-1, keepdims=True)
    rest = jnp.where(lane == i1, -jnp.inf, logits)
    m2 = jnp.max(rest, axis=-1, keepdims=True)
    i2 = jnp.min(jnp.where(rest == m2, lane, float(LANES)), axis=-1, keepdims=True)
    e2 = jnp.exp(m2 - m1)
    w1 = 1.0 / (1.0 + e2)
    w2 = e2 * w1
    o_ref[...] = jnp.where(lane == i1, w1, 0.0) + jnp.where(lane == i2, w2, 0.0)
    sel_ref[...] = (jnp.where(lane == 0, i1, 0.0) + jnp.where(lane == 1, i2, 0.0)
                    + jnp.where(lane == 2, w1, 0.0) + jnp.where(lane == 3, w2, 0.0))


def router_gates(x, w_router, tm):
    t, d = x.shape
    wpad = jnp.pad(w_router, ((0, 0), (0, LANES - N_EXPERTS)))
    out = pl.BlockSpec((tm, LANES), lambda i: (i, 0))
    return pl.pallas_call(
        _router_body, grid=(t // tm,),
        in_specs=[pl.BlockSpec((tm, d), lambda i: (i, 0)), pl.BlockSpec((d, LANES), lambda i: (0, 0))],
        out_specs=[out, out],
        out_shape=[jax.ShapeDtypeStruct((t, LANES), F32)] * 2, compiler_params=_cp("parallel"), name="router")(x, wpad)


def _moe_body(x_ref, gates_ref, wg_ref, wu_ref, wd_ref, o_ref, acc_s):
    e = pl.program_id(1)
    f = pl.program_id(2)

    @pl.when(jnp.logical_and(e == 0, f == 0))
    def _():
        acc_s[...] = jnp.zeros_like(acc_s)

    xb = x_ref[...].astype(BF16)
    g = jnp.dot(xb, wg_ref[...], preferred_element_type=F32)
    u = jnp.dot(xb, wu_ref[...], preferred_element_type=F32)
    gates = gates_ref[...]
    gate = jnp.sum(jnp.where(_col_iota(gates.shape) == e, gates, 0.0), axis=-1, keepdims=True)
    h = (_silu(g) * u * gate).astype(BF16)
    acc_s[...] += jnp.dot(h, wd_ref[...], preferred_element_type=F32)

    @pl.when(jnp.logical_and(e == pl.num_programs(1) - 1, f == pl.num_programs(2) - 1))
    def _():
        o_ref[...] = acc_s[...]


def moe_dense(x, gates, wg, wu, wd, tm, tf):
    t, d = x.shape
    ne, _, ff = wg.shape
    return pl.pallas_call(
        _moe_body, grid=(t // tm, ne, ff // tf),
        in_specs=[pl.BlockSpec((tm, d), lambda i, e, f: (i, 0)),
                  pl.BlockSpec((tm, LANES), lambda i, e, f: (i, 0)),
                  pl.BlockSpec((None, d, tf), lambda i, e, f: (e, 0, f)),
                  pl.BlockSpec((None, d, tf), lambda i, e, f: (e, 0, f)),
                  pl.BlockSpec((None, tf, d), lambda i, e, f: (e, f, 0))],
        out_specs=pl.BlockSpec((tm, d), lambda i, e, f: (i, 0)),
        out_shape=jax.ShapeDtypeStruct((t, d), F32),
        scratch_shapes=[pltpu.VMEM((tm, d), F32)],
        compiler_params=_cp("parallel", "arbitrary", "arbitrary"), name="moe_dense")(x, gates, wg, wu, wd)


def _dispatch_body(dest_ref, x_ref, zeros_ref, xs_ref, sem, *, tm):
    del zeros_ref

    def row_copy(r, s):
        return pltpu.make_async_copy(x_ref.at[pl.ds(r, 1)], xs_ref.at[pl.ds(dest_ref[2 * r + s], 1)], sem)

    def issue(r, carry):
        row_copy(r, 0).start()
        row_copy(r, 1).start()
        return carry

    def drain(r, carry):
        row_copy(r, 0).wait()
        row_copy(r, 1).wait()
        return carry

    lax.fori_loop(0, tm, issue, 0)
    lax.fori_loop(0, tm, drain, 0)


def moe_dispatch(x, dest, n_rows, tm):
    t, d = x.shape
    return pl.pallas_call(
        functools.partial(_dispatch_body, tm=tm), grid=(t // tm,),
        in_specs=[pl.BlockSpec((2 * tm,), lambda i: (i,), memory_space=pltpu.SMEM),
                  pl.BlockSpec((tm, d), lambda i: (i, 0)),
                  pl.BlockSpec(memory_space=pl.ANY)],
        out_specs=pl.BlockSpec(memory_space=pl.ANY),
        out_shape=jax.ShapeDtypeStruct((n_rows, d), F32),
        scratch_shapes=[pltpu.SemaphoreType.DMA(())],
        input_output_aliases={2: 0},
        compiler_params=_cp("arbitrary"), name="moe_dispatch")(dest, x, jnp.zeros((n_rows, d), F32))


def _moe_group_body(te_ref, nu_ref, x_ref, wg_ref, wu_ref, wd_ref, o_ref, acc_s):
    del te_ref
    i = pl.program_id(0)
    f = pl.program_id(1)
    last = f == pl.num_programs(1) - 1

    @pl.when(i < nu_ref[0])
    def _():
        xb = x_ref[...].astype(BF16)
        g = jnp.dot(xb, wg_ref[...], preferred_element_type=F32)
        u = jnp.dot(xb, wu_ref[...], preferred_element_type=F32)
        y = jnp.dot((_silu(g) * u).astype(BF16), wd_ref[...], preferred_element_type=F32)

        @pl.when(f == 0)
        def _():
            acc_s[...] = y

        @pl.when(f > 0)
        def _():
            acc_s[...] += y

        @pl.when(last)
        def _():
            o_ref[...] = acc_s[...]

    @pl.when(jnp.logical_and(i >= nu_ref[0], last))
    def _():
        o_ref[...] = jnp.zeros_like(o_ref)


def moe_grouped(xs, tile_expert, n_used, wg, wu, wd, tm, tf):
    r, d = xs.shape
    ff = wg.shape[2]
    grid_spec = pltpu.PrefetchScalarGridSpec(
        num_scalar_prefetch=2, grid=(r // tm, ff // tf),
        in_specs=[pl.BlockSpec((tm, d), lambda i, f, te, nu: (i, 0)),
                  pl.BlockSpec((None, d, tf), lambda i, f, te, nu: (te[i], 0, f)),
                  pl.BlockSpec((None, d, tf), lambda i, f, te, nu: (te[i], 0, f)),
                  pl.BlockSpec((None, tf, d), lambda i, f, te, nu: (te[i], f, 0))],
        out_specs=pl.BlockSpec((tm, d), lambda i, f, te, nu: (i, 0)),
        scratch_shapes=[pltpu.VMEM((tm, d), F32)])
    return pl.pallas_call(
        _moe_group_body, grid_spec=grid_spec, out_shape=jax.ShapeDtypeStruct((r, d), F32),
        compiler_params=_cp("arbitrary", "arbitrary"), name="moe_grouped")(tile_expert, n_used, xs, wg, wu, wd)


def _combine_body(dest_ref, sel_ref, ys_ref, o_ref, buf0, buf1, sem, *, tm):
    def row_copy(r, s):
        buf = buf0 if s == 0 else buf1
        return pltpu.make_async_copy(ys_ref.at[pl.ds(dest_ref[2 * r + s], 1)], buf.at[pl.ds(r, 1)], sem)

    def issue(r, carry):
        row_copy(r, 0).start()
        row_copy(r, 1).start()
        return carry

    def drain(r, carry):
        row_copy(r, 0).wait()
        row_copy(r, 1).wait()
        return carry

    lax.fori_loop(0, tm, issue, 0)
    lax.fori_loop(0, tm, drain, 0)
    sel = sel_ref[...]
    o_ref[...] = sel[:, 2:3] * buf0[...] + sel[:, 3:4] * buf1[...]


def moe_combine(ys, dest, sel, tm):
    t = sel.shape[0]
    d = ys.shape[1]
    return pl.pallas_call(
        functools.partial(_combine_body, tm=tm), grid=(t // tm,),
        in_specs=[pl.BlockSpec((2 * tm,), lambda i: (i,), memory_space=pltpu.SMEM),
                  pl.BlockSpec((tm, LANES), lambda i: (i, 0)),
                  pl.BlockSpec(memory_space=pl.ANY)],
        out_specs=pl.BlockSpec((tm, d), lambda i: (i, 0)),
        out_shape=jax.ShapeDtypeStruct((t, d), F32),
        scratch_shapes=[pltpu.VMEM((tm, d), F32), pltpu.VMEM((tm, d), F32), pltpu.SemaphoreType.DMA(())],
        compiler_params=_cp("arbitrary"), name="moe_combine")(dest, sel, ys)


def moe_sparse(x, sel, wg, wu, wd, tm, tf):
    t, _ = x.shape
    ne = wg.shape[0]
    e_flat = sel[:, :2].astype(jnp.int32).reshape(-1)
    onehot = (e_flat[:, None] == jnp.arange(ne, dtype=jnp.int32)[None, :]).astype(jnp.int32)
    csum = jnp.cumsum(onehot, axis=0)
    rank = jnp.sum(onehot * csum, axis=1) - 1
    counts = csum[-1]
    padded = ((counts + tm - 1) // tm) * tm
    ends = jnp.cumsum(padded)
    dest = (jnp.sum(onehot * (ends - padded)[None, :], axis=1) + rank).astype(jnp.int32)
    n_tiles = (2 * t) // tm + ne
    tile_start = jnp.arange(n_tiles, dtype=jnp.int32) * tm
    tile_expert = jnp.minimum(jnp.sum((tile_start[:, None] >= ends[None, :]).astype(jnp.int32), axis=1), ne - 1)
    n_used = (ends[-1:] // tm).astype(jnp.int32)
    xs = moe_dispatch(x, dest, n_tiles * tm, tm)
    ys = moe_grouped(xs, tile_expert.astype(jnp.int32), n_used, wg, wu, wd, tm, tf)
    return moe_combine(ys, dest, sel, tm)


def _ple_ln2_body(x_ref, f_ref, p_ref, wg_ref, wp_ref, g_ref, b_ref, o_ref):
    x1 = x_ref[...]
    gate = _sigmoid(jnp.dot(x1.astype(BF16), wg_ref[...], preferred_element_type=F32))
    proj = jnp.dot(p_ref[...].astype(BF16), wp_ref[...], preferred_element_type=F32)
    o_ref[...] = _layer_norm(DN_ALPHA * x1 + f_ref[...] + gate * proj, g_ref[...], b_ref[...])


def ple_ln2(x1, f, p, wg, wp, g, b, tm):
    t, d = x1.shape
    rows = lambda c: pl.BlockSpec((tm, c), lambda i: (i, 0))
    vec = pl.BlockSpec((1, d), lambda i: (0, 0))
    return pl.pallas_call(
        _ple_ln2_body, grid=(t // tm,),
        in_specs=[rows(d), rows(d), rows(PLE_DIM), pl.BlockSpec((d, d), lambda i: (0, 0)),
                  pl.BlockSpec((PLE_DIM, d), lambda i: (0, 0)), vec, vec],
        out_specs=rows(d), out_shape=jax.ShapeDtypeStruct((t, d), F32),
        compiler_params=_cp("parallel"), name="ple_ln2")(x1, f, p, wg, wp, g.reshape(1, d), b.reshape(1, d))


def _lru_body(x_ref, g_ref, prev0_ref, h0_ref, cw_ref, cb_ref, wg_ref, bg_ref, lam_ref,
              y_ref, ht_ref, prev_s, h_s, *, tl, first_is_pos0, valid_len):
    i = pl.program_id(1)

    @pl.when(i == 0)
    def _():
        prev_s[...] = prev0_ref[...]
        h_s[...] = h0_ref[...]

    u = x_ref[...]
    xc = _causal_conv(u, prev_s[...], cw_ref[...], cb_ref[...])
    prev_s[...] = u[tl - SUBLANES:tl]

    gates = jnp.dot(xc.astype(BF16), wg_ref[...], preferred_element_type=F32) + bg_ref[...]
    r = _sigmoid(gates[:, :LRU_WIDTH])
    ig = _sigmoid(gates[:, LRU_WIDTH:])
    a = jnp.exp(-LRU_C * r * _softplus(-lam_ref[...]))
    mult = jnp.sqrt(1.0 - a * a)
    row = i * tl + _row_iota(u.shape)
    if first_is_pos0:
        mult = jnp.where(row == 0, 1.0, mult)
    b = mult * ig * xc
    if valid_len is not None:
        valid = row < valid_len
        a = jnp.where(valid, a, 1.0)
        b = jnp.where(valid, b, 0.0)

    d = 1
    while d < tl:
        b = b + a * _shift_rows_fill(b, d, 0.0)
        a = a * _shift_rows_fill(a, d, 1.0)
        d *= 2
    hs = b + a * h_s[...]
    h_last = hs[tl - 1:tl]
    h_s[...] = h_last
    ht_ref[...] = h_last
    y_ref[...] = hs * _gelu_tanh(g_ref[...])


def lru_mixer(proj, nb, seq, tl, prev0, h0, cw, cb, wgate, bgate, lam, first_is_pos0, valid_len):
    nl = seq // tl
    w = LRU_WIDTH
    vec = lambda r, c: pl.BlockSpec((r, c), lambda b, i: (0, 0))
    body = functools.partial(_lru_body, tl=tl, first_is_pos0=first_is_pos0, valid_len=valid_len)
    return pl.pallas_call(
        body, grid=(nb, nl),
        in_specs=[pl.BlockSpec((tl, w), lambda b, i: (b * nl + i, C_XL // w)),
                  pl.BlockSpec((tl, w), lambda b, i: (b * nl + i, C_GL // w)),
                  pl.BlockSpec((None, SUBLANES, w), lambda b, i: (b, 0, 0)),
                  pl.BlockSpec((None, 1, w), lambda b, i: (b, 0, 0)),
                  vec(SUBLANES, w), vec(1, w), vec(w, 2 * w), vec(1, 2 * w), vec(1, w)],
        out_specs=[pl.BlockSpec((tl, w), lambda b, i: (b * nl + i, 0)),
                   pl.BlockSpec((None, 1, w), lambda b, i: (b, 0, 0))],
        out_shape=[jax.ShapeDtypeStruct((nb * seq, w), F32), jax.ShapeDtypeStruct((nb, 1, w), F32)],
        scratch_shapes=[pltpu.VMEM((SUBLANES, w), F32), pltpu.VMEM((1, w), F32)],
        compiler_params=_cp("parallel", "arbitrary"), name="lru")(
            proj, proj, prev0, h0, cw, cb, wgate, bgate, lam)


def _ssd_body(xbc_ref, z_ref, dt_ref, prev0_ref, s0_ref, cw_ref, cb_ref, dtb_ref, a_ref, d_ref, nw_ref,
              y_ref, st_ref, prev_s, s_s, *, rows, valid_len):
    i = pl.program_id(1)
    q = SSD_CHUNK
    hp = SSD_HEAD_DIM
    ns = SSD_D_STATE

    @pl.when(i == 0)
    def _():
        prev_s[...] = prev0_ref[...]
        s_s[...] = s0_ref[...]

    u = xbc_ref[...]
    xbc = _silu(_causal_conv(u, prev_s[...], cw_ref[...], cb_ref[...]))
    prev_s[...] = u[rows - SUBLANES:rows]
    dt = _softplus(dt_ref[...] + dtb_ref[...])
    if valid_len is not None:
        dt = jnp.where(i * rows + _row_iota(dt.shape) < valid_len, dt, 0.0)
    if rows < q:
        xbc = jnp.concatenate([xbc, jnp.zeros((q - rows, xbc.shape[1]), F32)], axis=0)
        dt = jnp.concatenate([dt, jnp.zeros((q - rows, dt.shape[1]), F32)], axis=0)
    xs = xbc[:, :SSD_D_INNER]
    bm = xbc[:, SSD_D_INNER:SSD_D_INNER + SSD_GROUPS * ns]
    cm = xbc[:, SSD_D_INNER + SSD_GROUPS * ns:]

    a_cs = dt * a_ref[...]
    d = 1
    while d < q:
        a_cs = a_cs + _shift_rows_fill(a_cs, d, 0.0)
        d *= 2
    a_cs_t = a_cs.T
    tri = _row_iota((q, q)) >= _col_iota((q, q))

    rep = SSD_HEADS // SSD_GROUPS
    nt = (((1,), (1,)), ((), ()))
    tn = (((0,), (0,)), ((), ()))
    cb_g = [lax.dot_general(cm[:, g * ns:(g + 1) * ns], bm[:, g * ns:(g + 1) * ns], nt,
                            preferred_element_type=F32) for g in range(SSD_GROUPS)]
    ys = []
    for h in range(SSD_HEADS):
        g = h // rep
        col = a_cs[:, h:h + 1]
        seg = col - a_cs_t[h:h + 1, :]
        lmat = jnp.exp(jnp.where(tri, seg, -jnp.inf))
        xs_h = xs[:, h * hp:(h + 1) * hp]
        xd = xs_h * dt[:, h:h + 1]
        c_g = cm[:, g * ns:(g + 1) * ns]
        b_g = bm[:, g * ns:(g + 1) * ns]
        s_prev = s_s[h]
        y_diag = jnp.dot(cb_g[g] * lmat, xd, preferred_element_type=F32)
        y_off = jnp.exp(col) * lax.dot_general(c_g, s_prev, nt, preferred_element_type=F32)
        a_last = a_cs[q - 1:q, h:h + 1]
        s_new = jnp.exp(a_last) * s_prev + lax.dot_general(xd * jnp.exp(a_last - col), b_g, tn,
                                                            preferred_element_type=F32)
        s_s[h] = s_new
        st_ref[h] = s_new
        ys.append(y_diag + y_off)
    y = jnp.concatenate(ys, axis=-1) + d_ref[...] * xs
    if rows < q:
        y = y[:rows]
    y = y * _silu(z_ref[...])
    y_ref[...] = y * lax.rsqrt(jnp.mean(y * y, axis=-1, keepdims=True) + RMS_EPS) * nw_ref[...]


def ssd_mixer(proj, nb, seq, rows, prev0, s0, cw, cb, dtb, a_neg, d_skip, norm_w, valid_len):
    nl = seq // rows
    cdim = SSD_CONV_DIM
    di = SSD_D_INNER
    vec = lambda r, c: pl.BlockSpec((r, c), lambda b, i: (0, 0))
    st = pl.BlockSpec((None, SSD_HEADS, SSD_HEAD_DIM, SSD_D_STATE), lambda b, i: (b, 0, 0, 0))
    body = functools.partial(_ssd_body, rows=rows, valid_len=valid_len)
    return pl.pallas_call(
        body, grid=(nb, nl),
        in_specs=[pl.BlockSpec((rows, cdim), lambda b, i: (b * nl + i, C_XBC // cdim)),
                  pl.BlockSpec((rows, di), lambda b, i: (b * nl + i, C_Z // di)),
                  pl.BlockSpec((rows, LANES), lambda b, i: (b * nl + i, C_DT // LANES)),
                  pl.BlockSpec((None, SUBLANES, cdim), lambda b, i: (b, 0, 0)),
                  st, vec(SUBLANES, cdim), vec(1, cdim), vec(1, LANES), vec(1, LANES), vec(1, di), vec(1, di)],
        out_specs=[pl.BlockSpec((rows, di), lambda b, i: (b * nl + i, 0)), st],
        out_shape=[jax.ShapeDtypeStruct((nb * seq, di), F32),
                   jax.ShapeDtypeStruct((nb, SSD_HEADS, SSD_HEAD_DIM, SSD_D_STATE), F32)],
        scratch_shapes=[pltpu.VMEM((SUBLANES, cdim), F32),
                        pltpu.VMEM((SSD_HEADS, SSD_HEAD_DIM, SSD_D_STATE), F32)],
        compiler_params=_cp("parallel", "arbitrary"), name="ssd")(
            proj, proj, proj, prev0, s0, cw, cb, dtb, a_neg, d_skip, norm_w)


def _neg_strict_upper(n):
    return jnp.where(_row_iota((n, n)) > _col_iota((n, n)), -1.0, 0.0).astype(BF16)


def _neg_abs(x):
    return pltpu.bitcast(pltpu.bitcast(x, jnp.uint32) | jnp.uint32(0x80000000), F32)


def _sb_prompt_body(bias_ref, q_ref, k_ref, v_ref, o_ref, c_s, rs_s, p_s, qm_s, *, tq):
    i = pl.program_id(1)
    hd = SB_HEAD_DIM
    neg_upper = _neg_strict_upper(tq)
    diag_mask = _col_iota((tq, tq)) < _row_iota((tq, tq))
    nt = (((1,), (1,)), ((), ()))
    H = SB_HEADS
    pw = 2 * hd
    low_half = _col_iota((tq, pw)) < hd
    for h in range(H):
        qp = q_ref[:, (h // 2) * pw:(h // 2 + 1) * pw]
        qm_s[h] = jnp.where(low_half if h % 2 == 0 else jnp.logical_not(low_half), qp, jnp.zeros_like(qp))

    def weights(kb, mask):
        r0 = pl.multiple_of(kb * tq, tq)
        zs = []
        for h in range(H):
            pcols = slice((h // 2) * pw, (h // 2 + 1) * pw)
            zs.append(lax.dot_general(qm_s[h], k_ref[pl.ds(r0, tq), pcols], nt,
                                      preferred_element_type=F32) + bias_ref[h] * LOG2E)
        sps, lbs = [], []
        for h in range(H):
            z = zs[h]
            sp = jnp.maximum(z, 0.0) + jnp.log2(1.0 + jnp.exp2(_neg_abs(z)))
            lbs.append(z - sp)
            if mask is not None:
                sp = jnp.where(mask, sp, 0.0)
            sps.append(sp)
        for h in range(H):
            sp = sps[h]
            later = jnp.dot(sp.astype(BF16), neg_upper, preferred_element_type=F32)
            p = jnp.exp2(lbs[h] + later)
            if mask is not None:
                p = jnp.where(mask, p, 0.0)
            p_s[h] = p.astype(BF16)
            rs_s[h] = jnp.sum(sp, axis=-1, keepdims=True)

    def accumulate(kb, first):
        r0 = pl.multiple_of(kb * tq, tq)
        for hp in range(H // 2):
            pcols = slice(hp * pw, (hp + 1) * pw)
            vp = v_ref[pl.ds(r0, tq), pcols]
            pv = []
            for h in (2 * hp, 2 * hp + 1):
                x = jnp.dot(p_s[h], vp, preferred_element_type=F32)
                if first:
                    c_s[h] = rs_s[h]
                else:
                    c = c_s[h]
                    x = jnp.exp2(-c) * x
                    c_s[h] = c + rs_s[h]
                pv.append(x)
            both = jnp.where(low_half, pv[0], pv[1])
            if first:
                o_ref[:, pcols] = both
            else:
                o_ref[:, pcols] += both

    weights(i, diag_mask)

    @pl.when(i > 0)
    def _():
        accumulate(i, True)
        weights(i - 1, None)

        def body(kk, carry):
            accumulate(i - kk + 1, False)
            weights(i - kk, None)
            return carry

        lax.fori_loop(2, i + 1, body, 0)
        accumulate(0, False)

    @pl.when(i == 0)
    def _():
        accumulate(0, True)


def sb_prompt(qkv, bias, nb, seq):
    tq = SB_TILE
    nq = seq // tq
    w = SB_WIDTH
    return pl.pallas_call(
        functools.partial(_sb_prompt_body, tq=tq), grid=(nb, nq),
        in_specs=[pl.BlockSpec(memory_space=pltpu.SMEM),
                  pl.BlockSpec((tq, w), lambda b, i: (b * nq + i, 0)),
                  pl.BlockSpec((seq, w), lambda b, i: (b, 1)),
                  pl.BlockSpec((seq, w), lambda b, i: (b, 2))],
        out_specs=pl.BlockSpec((tq, w), lambda b, i: (b * nq + i, 0)),
        out_shape=jax.ShapeDtypeStruct((nb * seq, w), F32),
        scratch_shapes=[pltpu.VMEM((SB_HEADS, tq, 1), F32), pltpu.VMEM((SB_HEADS, tq, 1), F32),
                        pltpu.VMEM((SB_HEADS, tq, tq), BF16),
                        pltpu.VMEM((SB_HEADS, tq, 2 * SB_HEAD_DIM), BF16)],
        compiler_params=_cp("parallel", "arbitrary"), name="sb_prompt")(bias, qkv, qkv, qkv)


def _sb_blocks_heads(q3, kts, vts, bias, neg_upper, mask):
    nh, nr, _ = q3.shape
    zs = [lax.dot_general(q3, kt, (((2,), (1,)), ((0,), (0,))), preferred_element_type=F32) + bias
          for kt in kts]
    sps, lbs = [], []
    for z in zs:
        sp = _softplus(z)
        lbs.append(z - sp)
        sps.append(sp if mask is None else jnp.where(mask, sp, 0.0))
    ps = []
    for sp, lb in zip(sps, lbs):
        nk = sp.shape[2]
        sp2 = sp.reshape(nh * nr, nk)
        hi = sp2.astype(BF16)
        lo = (sp2 - hi.astype(F32)).astype(BF16)
        later = (jnp.dot(hi, neg_upper, preferred_element_type=F32)
                 + jnp.dot(lo, neg_upper, preferred_element_type=F32)).reshape(nh, nr, nk)
        p = jnp.exp(lb + later)
        ps.append((p if mask is None else jnp.where(mask, p, 0.0)).astype(BF16))
    pvs = [lax.dot_general(p, vt, (((2,), (2,)), ((0,), (0,))), preferred_element_type=F32)
           for p, vt in zip(ps, vts)]
    return pvs, [jnp.sum(sp, axis=-1, keepdims=True) for sp in sps]


def _sb_sample_body(pt_ref, q_ref, bias_ref, kn_ref, vn_ref, *refs, npg):
    del pt_ref
    k_refs = refs[:npg]
    v_refs = refs[npg:2 * npg]
    o_ref, c_s, acc_s = refs[2 * npg:]
    j = pl.program_id(1)
    ps = PAGE_SIZE
    q3 = q_ref[...]
    bias = bias_ref[...]
    neg_upper = _neg_strict_upper(ps)

    @pl.when(j == 0)
    def _():
        shape = q3.shape[:2] + (ps,)
        mask = lax.broadcasted_iota(jnp.int32, shape, 2) < lax.broadcasted_iota(jnp.int32, shape, 1)
        pvs, rss = _sb_blocks_heads(q3, [kn_ref[...].astype(BF16)], [vn_ref[...].astype(BF16)], bias,
                                    neg_upper, mask)
        acc_s[...] = pvs[0]
        c_s[...] = rss[0]

    pvs, rss = _sb_blocks_heads(q3, [r[...].astype(BF16) for r in k_refs], [r[...].astype(BF16) for r in v_refs],
                                bias, neg_upper, None)
    c = c_s[...]
    acc = acc_s[...]
    for pv, rs in zip(pvs, rss):
        acc = acc + jnp.exp(-c) * pv
        c = c + rs
    c_s[...] = c
    acc_s[...] = acc

    @pl.when(j == pl.num_programs(1) - 1)
    def _():
        o_ref[...] = acc


def sb_sample(q3, bias, k_new_t, v_new_t, cache_k_t, cache_v_t, page_table, layer):
    nseq, npages = page_table.shape
    npg = PAGES_PER_STEP
    nsteps = npages // npg
    _, nh, nr, hd = q3.shape

    def page_spec(n):
        return pl.BlockSpec((None, None, nh, hd, PAGE_SIZE),
                            lambda b, j, pt: (layer, pt[b, npages - 1 - (j * npg + n)], 0, 0, 0))

    per_seq = lambda r, c: pl.BlockSpec((None, nh, r, c), lambda b, j, pt: (b, 0, 0, 0))
    grid_spec = pltpu.PrefetchScalarGridSpec(
        num_scalar_prefetch=1, grid=(nseq, nsteps),
        in_specs=[per_seq(nr, hd), pl.BlockSpec((nh, 1, 1), lambda b, j, pt: (0, 0, 0)),
                  per_seq(hd, PAGE_SIZE), per_seq(hd, PAGE_SIZE)]
                 + [page_spec(n) for n in range(npg)] * 2,
        out_specs=per_seq(nr, hd),
        scratch_shapes=[pltpu.VMEM((nh, nr, 1), F32), pltpu.VMEM((nh, nr, hd), F32)])
    return pl.pallas_call(
        functools.partial(_sb_sample_body, npg=npg), grid_spec=grid_spec,
        out_shape=jax.ShapeDtypeStruct((nseq, nh, nr, hd), F32),
        compiler_params=_cp("parallel", "arbitrary"), name="sb_sample")(
            page_table, q3, bias, k_new_t, v_new_t, *([cache_k_t] * npg), *([cache_v_t] * npg))


def _pad_prev(buf):
    return jnp.pad(buf, ((0, 0), (SUBLANES - (CONV_W - 1), 0), (0, 0)))


def _prep_weights(W):
    P = {}
    w_in = W['w_in']
    o = [0, 256, 512, 1024, 1536, 2048, 2304, 2816, 3328]
    P['w_in'] = jnp.concatenate(
        [w_in[:, :, o[0]:o[5]], w_in[:, :, o[6]:o[7]], w_in[:, :, o[5]:o[6]],
         jnp.pad(w_in[:, :, o[7]:o[8]], ((0, 0), (0, 0), (0, C_END - C_DT - SSD_HEADS)))], axis=-1).astype(BF16)
    eye = jnp.eye(LRU_BLOCKS, dtype=F32)
    bd = lambda w: jnp.einsum('lgij,gh->lgihj', w, eye).reshape(DEPTH, LRU_WIDTH, LRU_WIDTH)
    P['lru_wgate'] = jnp.concatenate([bd(W['lru_wa']), bd(W['lru_wx'])], axis=-1).astype(BF16)
    P['lru_bgate'] = jnp.concatenate([W['lru_ba'], W['lru_bx']], axis=-1)[:, None, :]
    pad_cw = lambda w: jnp.pad(w, ((0, 0), (0, SUBLANES - CONV_W), (0, 0)))
    P['lru_cw'] = pad_cw(W['lru_conv_w'])
    P['ssd_cw'] = pad_cw(W['ssd_conv_w'])
    lane_pad = lambda v: jnp.pad(v, ((0, 0), (0, LANES - SSD_HEADS)))[:, None, :]
    P['ssd_dtb'] = lane_pad(W['ssd_dt_bias'])
    P['ssd_a'] = lane_pad(-jnp.exp(W['ssd_a_log']))
    P['ssd_d'] = jnp.repeat(W['ssd_d'], SSD_HEAD_DIM, axis=-1)[:, None, :]
    for name in ('w_out', 'ffn_w_gate', 'ffn_w_up', 'ffn_w_down', 'moe_w_gate', 'moe_w_up', 'moe_w_down',
                 'ple_w_gate', 'ple_w_proj'):
        P[name] = W[name].astype(BF16)
    return P


def _run_group(x, p, nb, seq, valid_len, at_pos0, lru_h0, lru_buf0, ssd_s0, ssd_buf0, sb_fn, W, P):
    t = nb * seq
    tm = min(512, t)
    tl = min(256, seq)
    rows = min(SSD_CHUNK, seq)
    vl = None if valid_len == seq else valid_len
    x = ln_rows(x, W['ln_in_g'], W['ln_in_b'], tm)
    states = []
    for li in range(DEPTH):
        proj, qkv16 = in_proj(x, P['w_in'][li], min(256, t))
        y_lru, lru_ht = lru_mixer(proj, nb, seq, tl, _pad_prev(lru_buf0[li]), lru_h0[li][:, None, :],
                                  P['lru_cw'][li], W['lru_conv_b'][li][None], P['lru_wgate'][li],
                                  P['lru_bgate'][li], W['lru_lambda'][li][None], at_pos0, vl)
        y_ssd, ssd_st = ssd_mixer(proj, nb, seq, rows, _pad_prev(ssd_buf0[li]), ssd_s0[li],
                                  P['ssd_cw'][li], W['ssd_conv_b'][li][None], P['ssd_dtb'][li], P['ssd_a'][li],
                                  P['ssd_d'][li], W['ssd_norm_w'][li][None], vl)
        y_sb = sb_fn(li, proj, qkv16)
        x1 = outproj_ln1(y_lru, y_sb, y_ssd, x, P['w_out'][li], W['ln1_g'][li], W['ln1_b'][li], tm)
        j = li // 2
        if li % 2 == 0:
            f = ffn_dense(x1, P['ffn_w_gate'][j], P['ffn_w_up'][j], P['ffn_w_down'][j], tm, 1408)
        else:
            gates, sel = router_gates(x1, W['moe_router'][j], tm)
            if t >= MOE_SPARSE_MIN_TOKENS:
                f = moe_sparse(x1, sel, P['moe_w_gate'][j], P['moe_w_up'][j], P['moe_w_down'][j], tm, 1792)
            else:
                f = moe_dense(x1, gates, P['moe_w_gate'][j], P['moe_w_up'][j], P['moe_w_down'][j], tm, 1792)
        x = ple_ln2(x1, f, p[li], P['ple_w_gate'][li], P['ple_w_proj'][li], W['ln2_g'][li], W['ln2_b'][li], tm)
        p3 = proj.reshape(nb, seq, C_END)
        tail = slice(valid_len - (CONV_W - 1), valid_len)
        states.append((p3[:, :valid_len, C_K:C_K + SB_WIDTH].reshape(nb, valid_len, SB_HEADS, SB_HEAD_DIM),
                       p3[:, :valid_len, C_V:C_V + SB_WIDTH].reshape(nb, valid_len, SB_HEADS, SB_HEAD_DIM),
                       lru_ht[:, 0], p3[:, tail, C_XL:C_XL + LRU_WIDTH], ssd_st,
                       p3[:, tail, C_XBC:C_XBC + SSD_CONV_DIM]))
    stacked = tuple(jnp.stack([st[n] for st in states]) for n in range(6))
    return x, stacked


def kernel(x_prompt, x_sample, p_prompt, p_sample, cache_k, cache_v, page_table, state_lru_h, state_lru_conv, state_ssd, state_ssd_conv, ln_in_g, ln_in_b, w_in, lru_conv_w, lru_conv_b, lru_wa, lru_ba, lru_wx, lru_bx, lru_lambda, sb_bias, ssd_conv_w, ssd_conv_b, ssd_dt_bias, ssd_a_log, ssd_d, ssd_norm_w, w_out, ln1_g, ln1_b, ln2_g, ln2_b, ffn_w_gate, ffn_w_up, ffn_w_down, moe_router, moe_w_gate, moe_w_up, moe_w_down, ple_w_gate, ple_w_proj):
    W = dict(ln_in_g=ln_in_g, ln_in_b=ln_in_b, w_in=w_in, lru_conv_w=lru_conv_w, lru_conv_b=lru_conv_b,
             lru_wa=lru_wa, lru_ba=lru_ba, lru_wx=lru_wx, lru_bx=lru_bx, lru_lambda=lru_lambda,
             sb_bias=sb_bias, ssd_conv_w=ssd_conv_w, ssd_conv_b=ssd_conv_b, ssd_dt_bias=ssd_dt_bias,
             ssd_a_log=ssd_a_log, ssd_d=ssd_d, ssd_norm_w=ssd_norm_w, w_out=w_out, ln1_g=ln1_g,
             ln1_b=ln1_b, ln2_g=ln2_g, ln2_b=ln2_b, ffn_w_gate=ffn_w_gate, ffn_w_up=ffn_w_up,
             ffn_w_down=ffn_w_down, moe_router=moe_router, moe_w_gate=moe_w_gate, moe_w_up=moe_w_up,
             moe_w_down=moe_w_down, ple_w_gate=ple_w_gate, ple_w_proj=ple_w_proj)
    P = _prep_weights(W)
    d = D_MODEL

    bp, sp, _ = x_prompt.shape
    zeros = lambda *s: jnp.zeros((DEPTH, bp) + s, F32)

    def prompt_sb(li, proj, qkv16):
        del proj
        return sb_prompt(qkv16, sb_bias[li], bp, sp)

    y_prompt, st_p = _run_group(
        x_prompt.reshape(bp * sp, d), p_prompt.reshape(DEPTH, bp * sp, PLE_DIM), bp, sp, sp, True,
        zeros(LRU_WIDTH), zeros(CONV_W - 1, LRU_WIDTH), zeros(SSD_HEADS, SSD_HEAD_DIM, SSD_D_STATE),
        zeros(CONV_W - 1, SSD_CONV_DIM), prompt_sb, W, P)

    bs, ts, _ = x_sample.shape
    rs = SAMPLE_ROWS
    pad_rows = lambda a, ax: jnp.pad(a, [(0, rs - ts) if n == ax else (0, 0) for n in range(a.ndim)])
    scale = 1.0 / math.sqrt(SB_HEAD_DIM)
    cache_k_t = jnp.transpose(cache_k, (0, 1, 3, 4, 2))
    cache_v_t = jnp.transpose(cache_v, (0, 1, 3, 4, 2))
    row_valid = (jnp.arange(rs) < ts)[None, None, :, None]

    def sample_sb(li, proj, qkv16):
        del qkv16
        p3 = proj.reshape(bs, rs, C_END)
        heads = lambda c: p3[:, :, c:c + SB_WIDTH].reshape(bs, rs, SB_HEADS, SB_HEAD_DIM)
        q3 = jnp.where(row_valid, jnp.transpose(heads(C_Q), (0, 2, 1, 3)) * scale, 0.0).astype(BF16)
        new_t = lambda c: jnp.pad(jnp.transpose(heads(c), (0, 2, 3, 1)), ((0, 0),) * 3 + ((0, PAGE_SIZE - rs),))
        out = sb_sample(q3, sb_bias[li][:, None, None], new_t(C_K), new_t(C_V), cache_k_t, cache_v_t,
                        page_table, li)
        return jnp.transpose(out, (0, 2, 1, 3)).reshape(bs * rs, SB_WIDTH)

    y_sample, st_s = _run_group(
        pad_rows(x_sample, 1).reshape(bs * rs, d), pad_rows(p_sample, 2).reshape(DEPTH, bs * rs, PLE_DIM),
        bs, rs, ts, False, state_lru_h, state_lru_conv, state_ssd, state_ssd_conv, sample_sb, W, P)
    y_sample = y_sample.reshape(bs, rs, d)[:, :ts]

    return (y_prompt.reshape(bp, sp, d), y_sample) + st_p + st_s
```

```python
import functools
import math

import jax
import jax.numpy as jnp
from jax import lax
from jax.experimental import pallas as pl
from jax.experimental.pallas import tpu as pltpu

F32 = jnp.float32
BF16 = jnp.bfloat16

D_MODEL = 1024
DEPTH = 2
PAGE_SIZE = 128
LRU_WIDTH = 256
LRU_BLOCKS = 4
LRU_C = 8.0
SB_HEADS = 8
SB_HEAD_DIM = 64
SB_WIDTH = SB_HEADS * SB_HEAD_DIM
SSD_HEADS = 4
SSD_HEAD_DIM = 64
SSD_D_INNER = SSD_HEADS * SSD_HEAD_DIM
SSD_GROUPS = 2
SSD_D_STATE = 64
SSD_CONV_DIM = SSD_D_INNER + 2 * SSD_GROUPS * SSD_D_STATE
SSD_CHUNK = 128
CONV_W = 4
N_EXPERTS = 8
PLE_DIM = 256
DN_ALPHA = (2 * DEPTH) ** 0.25
LN_EPS = 1e-5
RMS_EPS = 1e-6
LOG2E = math.log2(math.e)

LANES = 128
SUBLANES = 8
VMEM_LIMIT = 56 * 1024 * 1024

C_XL, C_GL, C_Q, C_K, C_V, C_XBC, C_Z, C_DT, C_END = 0, 256, 512, 1024, 1536, 2048, 2560, 2816, 3072
SAMPLE_ROWS = 8
SB_TILE = 256
PAGES_PER_STEP = 16
MOE_SPARSE_MIN_TOKENS = 4096


def _cp(*sem):
    return pltpu.CompilerParams(dimension_semantics=sem, vmem_limit_bytes=VMEM_LIMIT)


def _layer_norm(x, g, b):
    mu = jnp.mean(x, axis=-1, keepdims=True)
    xc = x - mu
    var = jnp.mean(xc * xc, axis=-1, keepdims=True)
    return xc * lax.rsqrt(var + LN_EPS) * g + b


def _softplus(x):
    return jnp.maximum(x, 0.0) + jnp.log(1.0 + jnp.exp(-jnp.abs(x)))


def _sigmoid(x):
    return 1.0 / (1.0 + jnp.exp(-x))


def _silu(x):
    return x * _sigmoid(x)


def _gelu_tanh(x):
    return 0.5 * x * (1.0 + jnp.tanh(math.sqrt(2.0 / math.pi) * (x + 0.044715 * (x * x * x))))


def _row_iota(shape):
    return lax.broadcasted_iota(jnp.int32, shape, 0)


def _col_iota(shape):
    return lax.broadcasted_iota(jnp.int32, shape, 1)


def _shift_rows_prev(u, prev8, k):
    r = pltpu.roll(u, k, 0)
    head = jnp.where(_row_iota(prev8.shape) < k, pltpu.roll(prev8, k, 0), r[:SUBLANES])
    if u.shape[0] == SUBLANES:
        return head
    return jnp.concatenate([head, r[SUBLANES:]], axis=0)


def _causal_conv(u, prev8, cw, cb):
    out = cb + cw[3:4] * u
    for k in range(1, CONV_W):
        out = out + cw[CONV_W - 1 - k:CONV_W - k] * _shift_rows_prev(u, prev8, k)
    return out


def _shift_rows_fill(x, d, fill):
    return jnp.where(_row_iota(x.shape) >= d, pltpu.roll(x, d, 0), fill)


def _inproj_tail(x, w_ref, o_ref, qkv_ref):
    y = jnp.dot(x.astype(BF16), w_ref[...], preferred_element_type=F32)
    o_ref[...] = y
    qkv_ref[:, :SB_WIDTH] = (y[:, C_Q:C_Q + SB_WIDTH] * (LOG2E / math.sqrt(SB_HEAD_DIM))).astype(BF16)
    qkv_ref[:, SB_WIDTH:] = y[:, C_K:C_V + SB_WIDTH].astype(BF16)


def _inproj_body(x_ref, w_ref, o_ref, qkv_ref):
    _inproj_tail(x_ref[...], w_ref, o_ref, qkv_ref)


def _ln_inproj_body(x_ref, g_ref, b_ref, w_ref, xn_ref, o_ref, qkv_ref):
    xn = _layer_norm(x_ref[...], g_ref[...], b_ref[...])
    xn_ref[...] = xn
    _inproj_tail(xn, w_ref, o_ref, qkv_ref)


def in_proj(x, w, tm, ln=None):
    t, k = x.shape
    n = w.shape[1]
    rows = lambda c: pl.BlockSpec((tm, c), lambda i: (i, 0))
    const = lambda r, c: pl.BlockSpec((r, c), lambda i: (0, 0))
    outs = [(rows(n), jax.ShapeDtypeStruct((t, n), F32)), (rows(3 * SB_WIDTH), jax.ShapeDtypeStruct((t, 3 * SB_WIDTH), BF16))]
    if ln is None:
        body, ins, args = _inproj_body, [rows(k), const(k, n)], (x, w)
    else:
        body, ins, args = _ln_inproj_body, [rows(k), const(1, k), const(1, k), const(k, n)], (x, ln[0][None], ln[1][None], w)
        outs = [(rows(k), jax.ShapeDtypeStruct((t, k), F32))] + outs
    return pl.pallas_call(
        body, grid=(t // tm,), in_specs=ins, out_specs=[o[0] for o in outs], out_shape=[o[1] for o in outs],
        compiler_params=_cp("parallel"), name="in_proj")(*args)


def _outproj_body(yl_ref, ys_ref, yd_ref, x_ref, w_ref, g_ref, b_ref, o_ref):
    mixed = jnp.concatenate([yl_ref[...], ys_ref[...], yd_ref[...]], axis=-1).astype(BF16)
    mix = jnp.dot(mixed, w_ref[...], preferred_element_type=F32)
    o_ref[...] = _layer_norm(DN_ALPHA * x_ref[...] + mix, g_ref[...], b_ref[...])


def outproj_ln1(y_lru, y_sb, y_ssd, x, w, g, b, tm):
    t, d = x.shape
    rows = lambda c: pl.BlockSpec((tm, c), lambda i: (i, 0))
    vec = pl.BlockSpec((1, d), lambda i: (0, 0))
    return pl.pallas_call(
        _outproj_body, grid=(t // tm,),
        in_specs=[rows(LRU_WIDTH), rows(SB_WIDTH), rows(SSD_D_INNER), rows(d),
                  pl.BlockSpec((d, d), lambda i: (0, 0)), vec, vec],
        out_specs=rows(d), out_shape=jax.ShapeDtypeStruct((t, d), F32),
        compiler_params=_cp("parallel"), name="outproj_ln1")(y_lru, y_sb, y_ssd, x, w, g.reshape(1, d), b.reshape(1, d))


def _ffn_body(x_ref, wg_ref, wu_ref, wd_ref, p_ref, pg_ref, pp_ref, g_ref, b_ref, o_ref, acc_s):
    f = pl.program_id(1)

    @pl.when(f == 0)
    def _():
        acc_s[...] = jnp.zeros_like(acc_s)

    xb = x_ref[...].astype(BF16)
    g = jnp.dot(xb, wg_ref[...], preferred_element_type=F32)
    u = jnp.dot(xb, wu_ref[...], preferred_element_type=F32)
    h = (_silu(g) * u).astype(BF16)
    acc_s[...] += jnp.dot(h, wd_ref[...], preferred_element_type=F32)

    @pl.when(f == pl.num_programs(1) - 1)
    def _():
        o_ref[...] = _ple_ln2(x_ref[...], acc_s[...], p_ref[...], pg_ref[...], pp_ref[...], g_ref[...], b_ref[...])


def ffn_dense(x, wg, wu, wd, epilogue, tm, tf):
    t, d = x.shape
    ff = wg.shape[1]
    return pl.pallas_call(
        _ffn_body, grid=(t // tm, ff // tf),
        in_specs=[pl.BlockSpec((tm, d), lambda i, f: (i, 0)),
                  pl.BlockSpec((d, tf), lambda i, f: (0, f)),
                  pl.BlockSpec((d, tf), lambda i, f: (0, f)),
                  pl.BlockSpec((tf, d), lambda i, f: (f, 0))] + _epilogue_specs(tm, d, lambda i, f: (i, 0)),
        out_specs=pl.BlockSpec((tm, d), lambda i, f: (i, 0)),
        out_shape=jax.ShapeDtypeStruct((t, d), F32),
        scratch_shapes=[pltpu.VMEM((tm, d), F32)],
        compiler_params=_cp("parallel", "arbitrary"), name="ffn_dense")(x, wg, wu, wd, *epilogue)


def _router_body(x_ref, w_ref, o_ref, sel_ref, cnt_ref):
    logits = jnp.dot(x_ref[...], w_ref[...], preferred_element_type=F32, precision=lax.Precision.HIGHEST)
    lane = _col_iota(logits.shape).astype(F32)
    logits = jnp.where(lane < N_EXPERTS, logits, -jnp.inf)
    m1 = jnp.max(logits, axis=-1, keepdims=True)
    i1 = jnp.min(jnp.where(logits == m1, lane, float(LANES)), axis=-1, keepdims=True)
    rest = jnp.where(lane == i1, -jnp.inf, logits)
    m2 = jnp.max(rest, axis=-1, keepdims=True)
    i2 = jnp.min(jnp.where(rest == m2, lane, float(LANES)), axis=-1, keepdims=True)
    e2 = jnp.exp(m2 - m1)
    w1 = 1.0 / (1.0 + e2)
    w2 = e2 * w1
    o_ref[...] = jnp.where(lane == i1, w1, 0.0) + jnp.where(lane == i2, w2, 0.0)
    tm = logits.shape[0]
    hit1 = jnp.where(lane == i1, 1.0, 0.0)
    hit2 = jnp.where(lane == i2, 1.0, 0.0)
    member = hit1 + hit2
    before = jnp.where(_row_iota((tm, tm)) > _col_iota((tm, tm)), 1.0, 0.0).astype(BF16)
    earlier = jnp.dot(before, member.astype(BF16), preferred_element_type=F32)
    r1 = jnp.sum(hit1 * earlier, axis=-1, keepdims=True)
    r2 = jnp.sum(hit2 * earlier, axis=-1, keepdims=True)
    cnt_ref[...] = jnp.sum(member, axis=0, keepdims=True)
    sel_ref[...] = (jnp.where(lane == 0, i1, 0.0) + jnp.where(lane == 1, i2, 0.0)
                    + jnp.where(lane == 2, w1, 0.0) + jnp.where(lane == 3, w2, 0.0)
                    + jnp.where(lane == 4, r1, 0.0) + jnp.where(lane == 5, r2, 0.0))


def router_gates(x, w_router, tm):
    t, d = x.shape
    wpad = jnp.pad(w_router, ((0, 0), (0, LANES - N_EXPERTS)))
    out = pl.BlockSpec((tm, LANES), lambda i: (i, 0))
    return pl.pallas_call(
        _router_body, grid=(t // tm,),
        in_specs=[pl.BlockSpec((tm, d), lambda i: (i, 0)), pl.BlockSpec((d, LANES), lambda i: (0, 0))],
        out_specs=[out, out, pl.BlockSpec((None, 1, LANES), lambda i: (i, 0, 0))],
        out_shape=[jax.ShapeDtypeStruct((t, LANES), F32)] * 2 + [jax.ShapeDtypeStruct((t // tm, 1, LANES), F32)],
        compiler_params=_cp("parallel"), name="router")(x, wpad)


def _moe_body(x_ref, gates_ref, wg_ref, wu_ref, wd_ref, o_ref, acc_s):
    e = pl.program_id(1)
    f = pl.program_id(2)

    @pl.when(jnp.logical_and(e == 0, f == 0))
    def _():
        acc_s[...] = jnp.zeros_like(acc_s)

    xb = x_ref[...].astype(BF16)
    g = jnp.dot(xb, wg_ref[...], preferred_element_type=F32)
    u = jnp.dot(xb, wu_ref[...], preferred_element_type=F32)
    gates = gates_ref[...]
    gate = jnp.sum(jnp.where(_col_iota(gates.shape) == e, gates, 0.0), axis=-1, keepdims=True)
    h = (_silu(g) * u * gate).astype(BF16)
    acc_s[...] += jnp.dot(h, wd_ref[...], preferred_element_type=F32)

    @pl.when(jnp.logical_and(e == pl.num_programs(1) - 1, f == pl.num_programs(2) - 1))
    def _():
        o_ref[...] = acc_s[...]


def moe_dense(x, gates, wg, wu, wd, tm, tf):
    t, d = x.shape
    ne, _, ff = wg.shape
    return pl.pallas_call(
        _moe_body, grid=(t // tm, ne, ff // tf),
        in_specs=[pl.BlockSpec((tm, d), lambda i, e, f: (i, 0)),
                  pl.BlockSpec((tm, LANES), lambda i, e, f: (i, 0)),
                  pl.BlockSpec((None, d, tf), lambda i, e, f: (e, 0, f)),
                  pl.BlockSpec((None, d, tf), lambda i, e, f: (e, 0, f)),
                  pl.BlockSpec((None, tf, d), lambda i, e, f: (e, f, 0))],
        out_specs=pl.BlockSpec((tm, d), lambda i, e, f: (i, 0)),
        out_shape=jax.ShapeDtypeStruct((t, d), F32),
        scratch_shapes=[pltpu.VMEM((tm, d), F32)],
        compiler_params=_cp("parallel", "arbitrary", "arbitrary"), name="moe_dense")(x, gates, wg, wu, wd)


def _dispatch_body(dest_ref, x_ref, zeros_ref, xs_ref, sem, *, tm):
    del zeros_ref

    def row_copy(r, s):
        return pltpu.make_async_copy(x_ref.at[pl.ds(r, 1)], xs_ref.at[pl.ds(dest_ref[2 * r + s], 1)], sem)

    def issue(r, carry):
        row_copy(r, 0).start()
        row_copy(r, 1).start()
        return carry

    def drain(r, carry):
        row_copy(r, 0).wait()
        row_copy(r, 1).wait()
        return carry

    lax.fori_loop(0, tm, issue, 0, unroll=8)
    lax.fori_loop(0, tm, drain, 0, unroll=8)


def moe_dispatch(x, dest, n_rows, tm):
    t, d = x.shape
    return pl.pallas_call(
        functools.partial(_dispatch_body, tm=tm), grid=(t // tm,),
        in_specs=[pl.BlockSpec((2 * tm,), lambda i: (i,), memory_space=pltpu.SMEM),
                  pl.BlockSpec((tm, d), lambda i: (i, 0)),
                  pl.BlockSpec(memory_space=pl.ANY)],
        out_specs=pl.BlockSpec(memory_space=pl.ANY),
        out_shape=jax.ShapeDtypeStruct((n_rows, d), F32),
        scratch_shapes=[pltpu.SemaphoreType.DMA(())],
        input_output_aliases={2: 0},
        compiler_params=_cp("arbitrary"), name="moe_dispatch")(dest, x, jnp.zeros((n_rows, d), F32))


def _moe_group_body(te_ref, nu_ref, x_ref, wg_ref, wu_ref, wd_ref, o_ref, acc_s):
    del te_ref
    i = pl.program_id(0)
    f = pl.program_id(1)
    last = f == pl.num_programs(1) - 1

    @pl.when(i < nu_ref[0])
    def _():
        xb = x_ref[...].astype(BF16)
        g = jnp.dot(xb, wg_ref[...], preferred_element_type=F32)
        u = jnp.dot(xb, wu_ref[...], preferred_element_type=F32)
        y = jnp.dot((_silu(g) * u).astype(BF16), wd_ref[...], preferred_element_type=F32)

        @pl.when(f == 0)
        def _():
            acc_s[...] = y

        @pl.when(f > 0)
        def _():
            acc_s[...] += y

        @pl.when(last)
        def _():
            o_ref[...] = acc_s[...]

    @pl.when(jnp.logical_and(i >= nu_ref[0], last))
    def _():
        o_ref[...] = jnp.zeros_like(o_ref)


def moe_grouped(xs, tile_expert, n_used, wg, wu, wd, tm, tf):
    r, d = xs.shape
    ff = wg.shape[2]
    grid_spec = pltpu.PrefetchScalarGridSpec(
        num_scalar_prefetch=2, grid=(r // tm, ff // tf),
        in_specs=[pl.BlockSpec((tm, d), lambda i, f, te, nu: (i, 0)),
                  pl.BlockSpec((None, d, tf), lambda i, f, te, nu: (te[i], 0, f)),
                  pl.BlockSpec((None, d, tf), lambda i, f, te, nu: (te[i], 0, f)),
                  pl.BlockSpec((None, tf, d), lambda i, f, te, nu: (te[i], f, 0))],
        out_specs=pl.BlockSpec((tm, d), lambda i, f, te, nu: (i, 0)),
        scratch_shapes=[pltpu.VMEM((tm, d), F32)])
    return pl.pallas_call(
        _moe_group_body, grid_spec=grid_spec, out_shape=jax.ShapeDtypeStruct((r, d), F32),
        compiler_params=_cp("arbitrary", "arbitrary"), name="moe_grouped")(tile_expert, n_used, xs, wg, wu, wd)


def _combine_body(dest_ref, sel_ref, ys_ref, x_ref, p_ref, pg_ref, pp_ref, g_ref, b_ref, o_ref, buf0, buf1, sem,
                  *, tm):
    def row_copy(r, s):
        buf = buf0 if s == 0 else buf1
        return pltpu.make_async_copy(ys_ref.at[pl.ds(dest_ref[2 * r + s], 1)], buf.at[pl.ds(r, 1)], sem)

    def issue(r, carry):
        row_copy(r, 0).start()
        row_copy(r, 1).start()
        return carry

    def drain(r, carry):
        row_copy(r, 0).wait()
        row_copy(r, 1).wait()
        return carry

    lax.fori_loop(0, tm, issue, 0, unroll=8)
    lax.fori_loop(0, tm, drain, 0, unroll=8)
    sel = sel_ref[...]
    f = sel[:, 2:3] * buf0[...] + sel[:, 3:4] * buf1[...]
    o_ref[...] = _ple_ln2(x_ref[...], f, p_ref[...], pg_ref[...], pp_ref[...], g_ref[...], b_ref[...])


def moe_combine(ys, dest, sel, x1, tm, epilogue):
    t, d = x1.shape
    return pl.pallas_call(
        functools.partial(_combine_body, tm=tm), grid=(t // tm,),
        in_specs=[pl.BlockSpec((2 * tm,), lambda i: (i,), memory_space=pltpu.SMEM),
                  pl.BlockSpec((tm, LANES), lambda i: (i, 0)),
                  pl.BlockSpec(memory_space=pl.ANY),
                  pl.BlockSpec((tm, d), lambda i: (i, 0))] + _epilogue_specs(tm, d, lambda i: (i, 0)),
        out_specs=pl.BlockSpec((tm, d), lambda i: (i, 0)),
        out_shape=jax.ShapeDtypeStruct((t, d), F32),
        scratch_shapes=[pltpu.VMEM((tm, d), F32), pltpu.VMEM((tm, d), F32), pltpu.SemaphoreType.DMA(())],
        compiler_params=_cp("arbitrary"), name="moe_combine")(dest, sel, ys, x1, *epilogue)


def moe_sparse(x, sel, tile_counts, wg, wu, wd, tm, tf, epilogue):
    t, _ = x.shape
    ne = wg.shape[0]
    cnt = tile_counts[:, 0, :ne].astype(jnp.int32)
    before_tile = jnp.cumsum(cnt, axis=0) - cnt
    counts = jnp.sum(cnt, axis=0)
    padded = ((counts + tm - 1) // tm) * tm
    ends = jnp.cumsum(padded)
    base = jnp.repeat((ends - padded)[None, :] + before_tile, tm, axis=0)
    experts = jnp.arange(ne, dtype=jnp.int32)[None, :]

    def dest_of(slot):
        hit = sel[:, slot:slot + 1].astype(jnp.int32) == experts
        return jnp.sum(jnp.where(hit, base, 0), axis=1) + sel[:, 4 + slot].astype(jnp.int32)

    dest = jnp.stack([dest_of(0), dest_of(1)], axis=1).reshape(-1)
    n_tiles = (2 * t) // tm + ne
    tile_start = jnp.arange(n_tiles, dtype=jnp.int32) * tm
    tile_expert = jnp.minimum(jnp.sum((tile_start[:, None] >= ends[None, :]).astype(jnp.int32), axis=1), ne - 1)
    n_used = (ends[-1:] // tm).astype(jnp.int32)
    xs = moe_dispatch(x, dest, n_tiles * tm, tm)
    ys = moe_grouped(xs, tile_expert.astype(jnp.int32), n_used, wg, wu, wd, tm, tf)
    return moe_combine(ys, dest, sel, x, tm, epilogue)


def _ple_ln2(x1, f, p, wg, wp, g, b):
    gate = _sigmoid(jnp.dot(x1.astype(BF16), wg, preferred_element_type=F32))
    proj = jnp.dot(p.astype(BF16), wp, preferred_element_type=F32)
    return _layer_norm(DN_ALPHA * x1 + f + gate * proj, g, b)


def _ple_ln2_body(x_ref, f_ref, p_ref, wg_ref, wp_ref, g_ref, b_ref, o_ref):
    o_ref[...] = _ple_ln2(x_ref[...], f_ref[...], p_ref[...], wg_ref[...], wp_ref[...], g_ref[...], b_ref[...])


def _epilogue_specs(tm, d, idx):
    const = lambda *_: (0, 0)
    return [pl.BlockSpec((tm, PLE_DIM), idx), pl.BlockSpec((d, d), const), pl.BlockSpec((PLE_DIM, d), const),
            pl.BlockSpec((1, d), const), pl.BlockSpec((1, d), const)]


def ple_ln2(x1, f, epilogue, tm):
    t, d = x1.shape
    rows = lambda c: pl.BlockSpec((tm, c), lambda i: (i, 0))
    return pl.pallas_call(
        _ple_ln2_body, grid=(t // tm,),
        in_specs=[rows(d), rows(d)] + _epilogue_specs(tm, d, lambda i: (i, 0)),
        out_specs=rows(d), out_shape=jax.ShapeDtypeStruct((t, d), F32),
        compiler_params=_cp("parallel"), name="ple_ln2")(x1, f, *epilogue)


def _lru_body(x_ref, g_ref, prev0_ref, h0_ref, cw_ref, cb_ref, wg_ref, bg_ref, lam_ref,
              y_ref, ht_ref, prev_s, h_s, *, tl, first_is_pos0, valid_len):
    i = pl.program_id(1)

    @pl.when(i == 0)
    def _():
        prev_s[...] = prev0_ref[...]
        h_s[...] = h0_ref[...]

    u = x_ref[...]
    xc = _causal_conv(u, prev_s[...], cw_ref[...], cb_ref[...])
    prev_s[...] = u[tl - SUBLANES:tl]

    gates = jnp.dot(xc.astype(BF16), wg_ref[...], preferred_element_type=F32) + bg_ref[...]
    r = _sigmoid(gates[:, :LRU_WIDTH])
    ig = _sigmoid(gates[:, LRU_WIDTH:])
    a = jnp.exp(-LRU_C * r * _softplus(-lam_ref[...]))
    mult = jnp.sqrt(1.0 - a * a)
    row = i * tl + _row_iota(u.shape)
    if first_is_pos0:
        mult = jnp.where(row == 0, 1.0, mult)
    b = mult * ig * xc
    if valid_len is not None:
        valid = row < valid_len
        a = jnp.where(valid, a, 1.0)
        b = jnp.where(valid, b, 0.0)

    d = 1
    while d < tl:
        b = b + a * _shift_rows_fill(b, d, 0.0)
        a = a * _shift_rows_fill(a, d, 1.0)
        d *= 2
    hs = b + a * h_s[...]
    h_last = hs[tl - 1:tl]
    h_s[...] = h_last
    ht_ref[...] = h_last
    y_ref[...] = hs * _gelu_tanh(g_ref[...])


def lru_mixer(proj, nb, seq, tl, prev0, h0, cw, cb, wgate, bgate, lam, first_is_pos0, valid_len):
    nl = seq // tl
    w = LRU_WIDTH
    vec = lambda r, c: pl.BlockSpec((r, c), lambda b, i: (0, 0))
    body = functools.partial(_lru_body, tl=tl, first_is_pos0=first_is_pos0, valid_len=valid_len)
    return pl.pallas_call(
        body, grid=(nb, nl),
        in_specs=[pl.BlockSpec((tl, w), lambda b, i: (b * nl + i, C_XL // w)),
                  pl.BlockSpec((tl, w), lambda b, i: (b * nl + i, C_GL // w)),
                  pl.BlockSpec((None, SUBLANES, w), lambda b, i: (b, 0, 0)),
                  pl.BlockSpec((None, 1, w), lambda b, i: (b, 0, 0)),
                  vec(SUBLANES, w), vec(1, w), vec(w, 2 * w), vec(1, 2 * w), vec(1, w)],
        out_specs=[pl.BlockSpec((tl, w), lambda b, i: (b * nl + i, 0)),
                   pl.BlockSpec((None, 1, w), lambda b, i: (b, 0, 0))],
        out_shape=[jax.ShapeDtypeStruct((nb * seq, w), F32), jax.ShapeDtypeStruct((nb, 1, w), F32)],
        scratch_shapes=[pltpu.VMEM((SUBLANES, w), F32), pltpu.VMEM((1, w), F32)],
        compiler_params=_cp("parallel", "arbitrary"), name="lru")(
            proj, proj, prev0, h0, cw, cb, wgate, bgate, lam)


def _ssd_body(xbc_ref, z_ref, dt_ref, prev0_ref, s0_ref, cw_ref, cb_ref, dtb_ref, a_ref, d_ref, nw_ref,
              y_ref, st_ref, prev_s, s_s, *, rows, valid_len):
    i = pl.program_id(1)
    q = SSD_CHUNK
    hp = SSD_HEAD_DIM
    ns = SSD_D_STATE

    @pl.when(i == 0)
    def _():
        prev_s[...] = prev0_ref[...]
        s_s[...] = s0_ref[...]

    u = xbc_ref[...]
    xbc = _silu(_causal_conv(u, prev_s[...], cw_ref[...], cb_ref[...]))
    prev_s[...] = u[rows - SUBLANES:rows]
    dt = _softplus(dt_ref[...] + dtb_ref[...])
    if valid_len is not None:
        dt = jnp.where(i * rows + _row_iota(dt.shape) < valid_len, dt, 0.0)
    if rows < q:
        xbc = jnp.concatenate([xbc, jnp.zeros((q - rows, xbc.shape[1]), F32)], axis=0)
        dt = jnp.concatenate([dt, jnp.zeros((q - rows, dt.shape[1]), F32)], axis=0)
    xs = xbc[:, :SSD_D_INNER]
    bm = xbc[:, SSD_D_INNER:SSD_D_INNER + SSD_GROUPS * ns]
    cm = xbc[:, SSD_D_INNER + SSD_GROUPS * ns:]

    a_cs = dt * a_ref[...]
    d = 1
    while d < q:
        a_cs = a_cs + _shift_rows_fill(a_cs, d, 0.0)
        d *= 2
    a_cs_t = a_cs.T
    tri = _row_iota((q, q)) >= _col_iota((q, q))

    rep = SSD_HEADS // SSD_GROUPS
    nt = (((1,), (1,)), ((), ()))
    tn = (((0,), (0,)), ((), ()))
    cb_g = [lax.dot_general(cm[:, g * ns:(g + 1) * ns], bm[:, g * ns:(g + 1) * ns], nt,
                            preferred_element_type=F32) for g in range(SSD_GROUPS)]
    ys = []
    for h in range(SSD_HEADS):
        g = h // rep
        col = a_cs[:, h:h + 1]
        seg = col - a_cs_t[h:h + 1, :]
        lmat = jnp.exp(jnp.where(tri, seg, -jnp.inf))
        xs_h = xs[:, h * hp:(h + 1) * hp]
        xd = xs_h * dt[:, h:h + 1]
        c_g = cm[:, g * ns:(g + 1) * ns]
        b_g = bm[:, g * ns:(g + 1) * ns]
        s_prev = s_s[h]
        y_diag = jnp.dot(cb_g[g] * lmat, xd, preferred_element_type=F32)
        y_off = jnp.exp(col) * lax.dot_general(c_g, s_prev, nt, preferred_element_type=F32)
        a_last = a_cs[q - 1:q, h:h + 1]
        s_new = jnp.exp(a_last) * s_prev + lax.dot_general(xd * jnp.exp(a_last - col), b_g, tn,
                                                            preferred_element_type=F32)
        s_s[h] = s_new
        st_ref[h] = s_new
        ys.append(y_diag + y_off)
    y = jnp.concatenate(ys, axis=-1) + d_ref[...] * xs
    if rows < q:
        y = y[:rows]
    y = y * _silu(z_ref[...])
    y_ref[...] = y * lax.rsqrt(jnp.mean(y * y, axis=-1, keepdims=True) + RMS_EPS) * nw_ref[...]


def ssd_mixer(proj, nb, seq, rows, prev0, s0, cw, cb, dtb, a_neg, d_skip, norm_w, valid_len):
    nl = seq // rows
    cdim = SSD_CONV_DIM
    di = SSD_D_INNER
    vec = lambda r, c: pl.BlockSpec((r, c), lambda b, i: (0, 0))
    st = pl.BlockSpec((None, SSD_HEADS, SSD_HEAD_DIM, SSD_D_STATE), lambda b, i: (b, 0, 0, 0))
    body = functools.partial(_ssd_body, rows=rows, valid_len=valid_len)
    return pl.pallas_call(
        body, grid=(nb, nl),
        in_specs=[pl.BlockSpec((rows, cdim), lambda b, i: (b * nl + i, C_XBC // cdim)),
                  pl.BlockSpec((rows, di), lambda b, i: (b * nl + i, C_Z // di)),
                  pl.BlockSpec((rows, LANES), lambda b, i: (b * nl + i, C_DT // LANES)),
                  pl.BlockSpec((None, SUBLANES, cdim), lambda b, i: (b, 0, 0)),
                  st, vec(SUBLANES, cdim), vec(1, cdim), vec(1, LANES), vec(1, LANES), vec(1, di), vec(1, di)],
        out_specs=[pl.BlockSpec((rows, di), lambda b, i: (b * nl + i, 0)), st],
        out_shape=[jax.ShapeDtypeStruct((nb * seq, di), F32),
                   jax.ShapeDtypeStruct((nb, SSD_HEADS, SSD_HEAD_DIM, SSD_D_STATE), F32)],
        scratch_shapes=[pltpu.VMEM((SUBLANES, cdim), F32),
                        pltpu.VMEM((SSD_HEADS, SSD_HEAD_DIM, SSD_D_STATE), F32)],
        compiler_params=_cp("parallel", "arbitrary"), name="ssd")(
            proj, proj, proj, prev0, s0, cw, cb, dtb, a_neg, d_skip, norm_w)


def _neg_strict_upper(n):
    return jnp.where(_row_iota((n, n)) > _col_iota((n, n)), -1.0, 0.0).astype(BF16)


def _neg_abs(x):
    return pltpu.bitcast(pltpu.bitcast(x, jnp.uint32) | jnp.uint32(0x80000000), F32)


def _sb_prompt_body(bias_ref, q_ref, k_ref, v_ref, o_ref, c_s, rs_s, p_s, qm_s, *, tq):
    i = pl.program_id(1)
    hd = SB_HEAD_DIM
    neg_upper = _neg_strict_upper(tq)
    diag_mask = _col_iota((tq, tq)) < _row_iota((tq, tq))
    nt = (((1,), (1,)), ((), ()))
    H = SB_HEADS
    pw = 2 * hd
    low_half = _col_iota((tq, pw)) < hd
    lane_x = _col_iota((tq, pw))
    for h in range(H):
        qp = q_ref[:, (h // 2) * pw:(h // 2 + 1) * pw]
        qm = jnp.where(low_half if h % 2 == 0 else jnp.logical_not(low_half), qp, jnp.zeros_like(qp))
        b0 = jnp.full((tq, pw), bias_ref[h] * LOG2E, F32)
        b_hi = b0.astype(BF16).astype(F32)
        b_mid = (b0 - b_hi).astype(BF16).astype(F32)
        b_lo = b0 - b_hi - b_mid
        ext = jnp.where(lane_x == 0, b_hi, jnp.where(lane_x == 1, b_mid, jnp.where(lane_x == 2, b_lo, 0.0)))
        qm_s[h] = jnp.concatenate([qm, ext.astype(BF16)], axis=1)
    ones_ext = jnp.ones((tq, pw), BF16)

    def weights(kb, mask):
        r0 = pl.multiple_of(kb * tq, tq)
        zs = []
        for h in range(H):
            pcols = slice((h // 2) * pw, (h // 2 + 1) * pw)
            k_ext = jnp.concatenate([k_ref[pl.ds(r0, tq), pcols], ones_ext], axis=1)
            zs.append(lax.dot_general(qm_s[h], k_ext, nt, preferred_element_type=F32))
        sps, lbs = [], []
        for h in range(H):
            z = zs[h]
            sp = jnp.maximum(z, 0.0) + jnp.log2(1.0 + jnp.exp2(_neg_abs(z)))
            lbs.append(z - sp)
            if mask is not None:
                sp = jnp.where(mask, sp, 0.0)
            sps.append(sp)
        for h in range(H):
            sp = sps[h]
            later = jnp.dot(sp.astype(BF16), neg_upper, preferred_element_type=F32)
            p = jnp.exp2(lbs[h] + later)
            if mask is not None:
                p = jnp.where(mask, p, 0.0)
            p_s[h] = p.astype(BF16)
            rs_s[h] = jnp.sum(sp, axis=-1, keepdims=True)

    def accumulate(kb, first):
        r0 = pl.multiple_of(kb * tq, tq)
        for hp in range(H // 2):
            pcols = slice(hp * pw, (hp + 1) * pw)
            vp = v_ref[pl.ds(r0, tq), pcols]
            pv = []
            for h in (2 * hp, 2 * hp + 1):
                x = jnp.dot(p_s[h], vp, preferred_element_type=F32)
                if first:
                    c_s[h] = rs_s[h]
                else:
                    c = c_s[h]
                    x = jnp.exp2(-c) * x
                    c_s[h] = c + rs_s[h]
                pv.append(x)
            both = jnp.where(low_half, pv[0], pv[1])
            if first:
                o_ref[:, pcols] = both
            else:
                o_ref[:, pcols] += both

    weights(i, diag_mask)

    @pl.when(i > 0)
    def _():
        accumulate(i, True)
        weights(i - 1, None)

        def body(kk, carry):
            accumulate(i - kk + 1, False)
            weights(i - kk, None)
            return carry

        lax.fori_loop(2, i + 1, body, 0)
        accumulate(0, False)

    @pl.when(i == 0)
    def _():
        accumulate(0, True)


def sb_prompt(qkv, bias, nb, seq):
    tq = SB_TILE
    nq = seq // tq
    w = SB_WIDTH
    return pl.pallas_call(
        functools.partial(_sb_prompt_body, tq=tq), grid=(nb, nq),
        in_specs=[pl.BlockSpec(memory_space=pltpu.SMEM),
                  pl.BlockSpec((tq, w), lambda b, i: (b * nq + i, 0)),
                  pl.BlockSpec((seq, w), lambda b, i: (b, 1)),
                  pl.BlockSpec((seq, w), lambda b, i: (b, 2))],
        out_specs=pl.BlockSpec((tq, w), lambda b, i: (b * nq + i, 0)),
        out_shape=jax.ShapeDtypeStruct((nb * seq, w), F32),
        scratch_shapes=[pltpu.VMEM((SB_HEADS, tq, 1), F32), pltpu.VMEM((SB_HEADS, tq, 1), F32),
                        pltpu.VMEM((SB_HEADS, tq, tq), BF16),
                        pltpu.VMEM((SB_HEADS, tq, 4 * SB_HEAD_DIM), BF16)],
        compiler_params=_cp("parallel", "arbitrary"), name="sb_prompt")(bias, qkv, qkv, qkv)


def _sb_blocks_heads(q3, kts, vts, bias, neg_upper, mask):
    nh, nr, _ = q3.shape
    zs = [lax.dot_general(q3, kt, (((2,), (1,)), ((0,), (0,))), preferred_element_type=F32) + bias
          for kt in kts]
    sps, lbs = [], []
    for z in zs:
        sp = _softplus(z)
        lbs.append(z - sp)
        sps.append(sp if mask is None else jnp.where(mask, sp, 0.0))
    ps = []
    for sp, lb in zip(sps, lbs):
        nk = sp.shape[2]
        sp2 = sp.reshape(nh * nr, nk)
        hi = sp2.astype(BF16)
        lo = (sp2 - hi.astype(F32)).astype(BF16)
        later = (jnp.dot(hi, neg_upper, preferred_element_type=F32)
                 + jnp.dot(lo, neg_upper, preferred_element_type=F32)).reshape(nh, nr, nk)
        p = jnp.exp(lb + later)
        ps.append((p if mask is None else jnp.where(mask, p, 0.0)).astype(BF16))
    pvs = [lax.dot_general(p, vt, (((2,), (2,)), ((0,), (0,))), preferred_element_type=F32)
           for p, vt in zip(ps, vts)]
    return pvs, [jnp.sum(sp, axis=-1, keepdims=True) for sp in sps]


def _sb_sample_body(pt_ref, q_ref, bias_ref, kn_ref, vn_ref, *refs, npg):
    del pt_ref
    k_refs = refs[:npg]
    v_refs = refs[npg:2 * npg]
    o_ref, c_s, acc_s = refs[2 * npg:]
    j = pl.program_id(1)
    ps = PAGE_SIZE
    q3 = q_ref[...]
    bias = bias_ref[...]
    neg_upper = _neg_strict_upper(ps)

    @pl.when(j == 0)
    def _():
        shape = q3.shape[:2] + (ps,)
        mask = lax.broadcasted_iota(jnp.int32, shape, 2) < lax.broadcasted_iota(jnp.int32, shape, 1)
        pvs, rss = _sb_blocks_heads(q3, [kn_ref[...].astype(BF16)], [vn_ref[...].astype(BF16)], bias,
                                    neg_upper, mask)
        acc_s[...] = pvs[0]
        c_s[...] = rss[0]

    pvs, rss = _sb_blocks_heads(q3, [r[...].astype(BF16) for r in k_refs], [r[...].astype(BF16) for r in v_refs],
                                bias, neg_upper, None)
    c = c_s[...]
    acc = acc_s[...]
    for pv, rs in zip(pvs, rss):
        acc = acc + jnp.exp(-c) * pv
        c = c + rs
    c_s[...] = c
    acc_s[...] = acc

    @pl.when(j == pl.num_programs(1) - 1)
    def _():
        o_ref[...] = acc


def sb_sample(q3, bias, k_new_t, v_new_t, cache_k_t, cache_v_t, page_table, layer):
    nseq, npages = page_table.shape
    npg = PAGES_PER_STEP
    nsteps = npages // npg
    _, nh, nr, hd = q3.shape

    def page_spec(n):
        return pl.BlockSpec((None, None, nh, hd, PAGE_SIZE),
                            lambda b, j, pt: (layer, pt[b, npages - 1 - (j * npg + n)], 0, 0, 0))

    per_seq = lambda r, c: pl.BlockSpec((None, nh, r, c), lambda b, j, pt: (b, 0, 0, 0))
    grid_spec = pltpu.PrefetchScalarGridSpec(
        num_scalar_prefetch=1, grid=(nseq, nsteps),
        in_specs=[per_seq(nr, hd), pl.BlockSpec((nh, 1, 1), lambda b, j, pt: (0, 0, 0)),
                  per_seq(hd, PAGE_SIZE), per_seq(hd, PAGE_SIZE)]
                 + [page_spec(n) for n in range(npg)] * 2,
        out_specs=per_seq(nr, hd),
        scratch_shapes=[pltpu.VMEM((nh, nr, 1), F32), pltpu.VMEM((nh, nr, hd), F32)])
    return pl.pallas_call(
        functools.partial(_sb_sample_body, npg=npg), grid_spec=grid_spec,
        out_shape=jax.ShapeDtypeStruct((nseq, nh, nr, hd), F32),
        compiler_params=_cp("parallel", "arbitrary"), name="sb_sample")(
            page_table, q3, bias, k_new_t, v_new_t, *([cache_k_t] * npg), *([cache_v_t] * npg))


def _pad_prev(buf):
    return jnp.pad(buf, ((0, 0), (SUBLANES - (CONV_W - 1), 0), (0, 0)))


def _prep_weights(W):
    P = {}
    w_in = W['w_in']
    o = [0, 256, 512, 1024, 1536, 2048, 2304, 2816, 3328]
    P['w_in'] = jnp.concatenate(
        [w_in[:, :, o[0]:o[5]], w_in[:, :, o[6]:o[7]], w_in[:, :, o[5]:o[6]],
         jnp.pad(w_in[:, :, o[7]:o[8]], ((0, 0), (0, 0), (0, C_END - C_DT - SSD_HEADS)))], axis=-1).astype(BF16)
    eye = jnp.eye(LRU_BLOCKS, dtype=F32)
    bd = lambda w: jnp.einsum('lgij,gh->lgihj', w, eye).reshape(DEPTH, LRU_WIDTH, LRU_WIDTH)
    P['lru_wgate'] = jnp.concatenate([bd(W['lru_wa']), bd(W['lru_wx'])], axis=-1).astype(BF16)
    P['lru_bgate'] = jnp.concatenate([W['lru_ba'], W['lru_bx']], axis=-1)[:, None, :]
    pad_cw = lambda w: jnp.pad(w, ((0, 0), (0, SUBLANES - CONV_W), (0, 0)))
    P['lru_cw'] = pad_cw(W['lru_conv_w'])
    P['ssd_cw'] = pad_cw(W['ssd_conv_w'])
    lane_pad = lambda v: jnp.pad(v, ((0, 0), (0, LANES - SSD_HEADS)))[:, None, :]
    P['ssd_dtb'] = lane_pad(W['ssd_dt_bias'])
    P['ssd_a'] = lane_pad(-jnp.exp(W['ssd_a_log']))
    P['ssd_d'] = jnp.repeat(W['ssd_d'], SSD_HEAD_DIM, axis=-1)[:, None, :]
    for name in ('w_out', 'ffn_w_gate', 'ffn_w_up', 'ffn_w_down', 'moe_w_gate', 'moe_w_up', 'moe_w_down',
                 'ple_w_gate', 'ple_w_proj'):
        P[name] = W[name].astype(BF16)
    return P


def _run_group(x, p, nb, seq, valid_len, at_pos0, lru_h0, lru_buf0, ssd_s0, ssd_buf0, sb_fn, W, P):
    t = nb * seq
    tm = min(512, t)
    tl = min(256, seq)
    rows = min(SSD_CHUNK, seq)
    vl = None if valid_len == seq else valid_len
    states = []
    for li in range(DEPTH):
        if li == 0:
            x, proj, qkv16 = in_proj(x, P['w_in'][li], min(256, t), (W['ln_in_g'], W['ln_in_b']))
        else:
            proj, qkv16 = in_proj(x, P['w_in'][li], min(256, t))
        y_lru, lru_ht = lru_mixer(proj, nb, seq, tl, _pad_prev(lru_buf0[li]), lru_h0[li][:, None, :],
                                  P['lru_cw'][li], W['lru_conv_b'][li][None], P['lru_wgate'][li],
                                  P['lru_bgate'][li], W['lru_lambda'][li][None], at_pos0, vl)
        y_ssd, ssd_st = ssd_mixer(proj, nb, seq, rows, _pad_prev(ssd_buf0[li]), ssd_s0[li],
                                  P['ssd_cw'][li], W['ssd_conv_b'][li][None], P['ssd_dtb'][li], P['ssd_a'][li],
                                  P['ssd_d'][li], W['ssd_norm_w'][li][None], vl)
        y_sb = sb_fn(li, proj, qkv16)
        x1 = outproj_ln1(y_lru, y_sb, y_ssd, x, P['w_out'][li], W['ln1_g'][li], W['ln1_b'][li], tm)
        epilogue = (p[li], P['ple_w_gate'][li], P['ple_w_proj'][li], W['ln2_g'][li][None], W['ln2_b'][li][None])
        j = li // 2
        if li % 2 == 0:
            x = ffn_dense(x1, P['ffn_w_gate'][j], P['ffn_w_up'][j], P['ffn_w_down'][j], epilogue, tm, 1408)
        else:
            gates, sel, tile_counts = router_gates(x1, W['moe_router'][j], tm)
            if t >= MOE_SPARSE_MIN_TOKENS:
                x = moe_sparse(x1, sel, tile_counts, P['moe_w_gate'][j], P['moe_w_up'][j], P['moe_w_down'][j],
                               tm, 1792, epilogue)
            else:
                f = moe_dense(x1, gates, P['moe_w_gate'][j], P['moe_w_up'][j], P['moe_w_down'][j], tm, 1792)
                x = ple_ln2(x1, f, epilogue, tm)
        p3 = proj.reshape(nb, seq, C_END)
        tail = slice(valid_len - (CONV_W - 1), valid_len)
        states.append((p3[:, :valid_len, C_K:C_K + SB_WIDTH].reshape(nb, valid_len, SB_HEADS, SB_HEAD_DIM),
                       p3[:, :valid_len, C_V:C_V + SB_WIDTH].reshape(nb, valid_len, SB_HEADS, SB_HEAD_DIM),
                       lru_ht[:, 0], p3[:, tail, C_XL:C_XL + LRU_WIDTH], ssd_st,
                       p3[:, tail, C_XBC:C_XBC + SSD_CONV_DIM]))
    stacked = tuple(jnp.stack([st[n] for st in states]) for n in range(6))
    return x, stacked


def kernel(x_prompt, x_sample, p_prompt, p_sample, cache_k, cache_v, page_table, state_lru_h, state_lru_conv, state_ssd, state_ssd_conv, ln_in_g, ln_in_b, w_in, lru_conv_w, lru_conv_b, lru_wa, lru_ba, lru_wx, lru_bx, lru_lambda, sb_bias, ssd_conv_w, ssd_conv_b, ssd_dt_bias, ssd_a_log, ssd_d, ssd_norm_w, w_out, ln1_g, ln1_b, ln2_g, ln2_b, ffn_w_gate, ffn_w_up, ffn_w_down, moe_router, moe_w_gate, moe_w_up, moe_w_down, ple_w_gate, ple_w_proj):
    W = dict(ln_in_g=ln_in_g, ln_in_b=ln_in_b, w_in=w_in, lru_conv_w=lru_conv_w, lru_conv_b=lru_conv_b,
             lru_wa=lru_wa, lru_ba=lru_ba, lru_wx=lru_wx, lru_bx=lru_bx, lru_lambda=lru_lambda,
             sb_bias=sb_bias, ssd_conv_w=ssd_conv_w, ssd_conv_b=ssd_conv_b, ssd_dt_bias=ssd_dt_bias,
             ssd_a_log=ssd_a_log, ssd_d=ssd_d, ssd_norm_w=ssd_norm_w, w_out=w_out, ln1_g=ln1_g,
             ln1_b=ln1_b, ln2_g=ln2_g, ln2_b=ln2_b, ffn_w_gate=ffn_w_gate, ffn_w_up=ffn_w_up,
             ffn_w_down=ffn_w_down, moe_router=moe_router, moe_w_gate=moe_w_gate, moe_w_up=moe_w_up,
             moe_w_down=moe_w_down, ple_w_gate=ple_w_gate, ple_w_proj=ple_w_proj)
    P = _prep_weights(W)
    d = D_MODEL

    bp, sp, _ = x_prompt.shape
    zeros = lambda *s: jnp.zeros((DEPTH, bp) + s, F32)

    def prompt_sb(li, proj, qkv16):
        del proj
        return sb_prompt(qkv16, sb_bias[li], bp, sp)

    y_prompt, st_p = _run_group(
        x_prompt.reshape(bp * sp, d), p_prompt.reshape(DEPTH, bp * sp, PLE_DIM), bp, sp, sp, True,
        zeros(LRU_WIDTH), zeros(CONV_W - 1, LRU_WIDTH), zeros(SSD_HEADS, SSD_HEAD_DIM, SSD_D_STATE),
        zeros(CONV_W - 1, SSD_CONV_DIM), prompt_sb, W, P)

    bs, ts, _ = x_sample.shape
    rs = SAMPLE_ROWS
    pad_rows = lambda a, ax: jnp.pad(a, [(0, rs - ts) if n == ax else (0, 0) for n in range(a.ndim)])
    scale = 1.0 / math.sqrt(SB_HEAD_DIM)
    cache_k_t = jnp.transpose(cache_k, (0, 1, 3, 4, 2))
    cache_v_t = jnp.transpose(cache_v, (0, 1, 3, 4, 2))
    row_valid = (jnp.arange(rs) < ts)[None, None, :, None]

    def sample_sb(li, proj, qkv16):
        del qkv16
        p3 = proj.reshape(bs, rs, C_END)
        heads = lambda c: p3[:, :, c:c + SB_WIDTH].reshape(bs, rs, SB_HEADS, SB_HEAD_DIM)
        q3 = jnp.where(row_valid, jnp.transpose(heads(C_Q), (0, 2, 1, 3)) * scale, 0.0).astype(BF16)
        new_t = lambda c: jnp.pad(jnp.transpose(heads(c), (0, 2, 3, 1)), ((0, 0),) * 3 + ((0, PAGE_SIZE - rs),))
        out = sb_sample(q3, sb_bias[li][:, None, None], new_t(C_K), new_t(C_V), cache_k_t, cache_v_t,
                        page_table, li)
        return jnp.transpose(out, (0, 2, 1, 3)).reshape(bs * rs, SB_WIDTH)

    y_sample, st_s = _run_group(
        pad_rows(x_sample, 1).reshape(bs * rs, d), pad_rows(p_sample, 2).reshape(DEPTH, bs * rs, PLE_DIM),
        bs, rs, ts, False, state_lru_h, state_lru_conv, state_ssd, state_ssd_conv, sample_sb, W, P)
    y_sample = y_sample.reshape(bs, rs, d)[:, :ts]

    return (y_prompt.reshape(bp, sp, d), y_sample) + st_p + st_s
```

```python
import functools
import math

import jax
import jax.numpy as jnp
from jax import lax
from jax.experimental import pallas as pl
from jax.experimental.pallas import tpu as pltpu

F32 = jnp.float32
BF16 = jnp.bfloat16

D_MODEL = 1024
DEPTH = 2
PAGE_SIZE = 128
LRU_WIDTH = 256
LRU_BLOCKS = 4
LRU_C = 8.0
SB_HEADS = 8
SB_HEAD_DIM = 64
SB_WIDTH = SB_HEADS * SB_HEAD_DIM
SSD_HEADS = 4
SSD_HEAD_DIM = 64
SSD_D_INNER = SSD_HEADS * SSD_HEAD_DIM
SSD_GROUPS = 2
SSD_D_STATE = 64
SSD_CONV_DIM = SSD_D_INNER + 2 * SSD_GROUPS * SSD_D_STATE
SSD_CHUNK = 128
CONV_W = 4
N_EXPERTS = 8
PLE_DIM = 256
DN_ALPHA = (2 * DEPTH) ** 0.25
LN_EPS = 1e-5
RMS_EPS = 1e-6
LOG2E = math.log2(math.e)

LANES = 128
SUBLANES = 8
VMEM_LIMIT = 56 * 1024 * 1024

C_XL, C_GL, C_Q, C_K, C_V, C_XBC, C_Z, C_DT, C_END = 0, 256, 512, 1024, 1536, 2048, 2560, 2816, 3072
SAMPLE_ROWS = 8
SB_TILE = 256
PAGES_PER_STEP = 16
MOE_SPARSE_MIN_TOKENS = 4096


def _cp(*sem):
    return pltpu.CompilerParams(dimension_semantics=sem, vmem_limit_bytes=VMEM_LIMIT)


def _layer_norm(x, g, b):
    mu = jnp.mean(x, axis=-1, keepdims=True)
    xc = x - mu
    var = jnp.mean(xc * xc, axis=-1, keepdims=True)
    return xc * lax.rsqrt(var + LN_EPS) * g + b


def _softplus(x):
    return jnp.maximum(x, 0.0) + jnp.log(1.0 + jnp.exp(-jnp.abs(x)))


def _sigmoid(x):
    return 1.0 / (1.0 + jnp.exp(-x))


def _silu(x):
    return x * _sigmoid(x)


def _gelu_tanh(x):
    return 0.5 * x * (1.0 + jnp.tanh(math.sqrt(2.0 / math.pi) * (x + 0.044715 * (x * x * x))))


def _row_iota(shape):
    return lax.broadcasted_iota(jnp.int32, shape, 0)


def _col_iota(shape):
    return lax.broadcasted_iota(jnp.int32, shape, 1)


def _shift_rows_prev(u, prev8, k):
    r = pltpu.roll(u, k, 0)
    head = jnp.where(_row_iota(prev8.shape) < k, pltpu.roll(prev8, k, 0), r[:SUBLANES])
    if u.shape[0] == SUBLANES:
        return head
    return jnp.concatenate([head, r[SUBLANES:]], axis=0)


def _causal_conv(u, prev8, cw, cb):
    out = cb + cw[3:4] * u
    for k in range(1, CONV_W):
        out = out + cw[CONV_W - 1 - k:CONV_W - k] * _shift_rows_prev(u, prev8, k)
    return out


def _shift_rows_fill(x, d, fill):
    return jnp.where(_row_iota(x.shape) >= d, pltpu.roll(x, d, 0), fill)


def _inproj_tail(x, w_ref, o_ref, qkv_ref):
    y = jnp.dot(x.astype(BF16), w_ref[...], preferred_element_type=F32)
    o_ref[...] = y
    qkv_ref[:, :SB_WIDTH] = (y[:, C_Q:C_Q + SB_WIDTH] * (LOG2E / math.sqrt(SB_HEAD_DIM))).astype(BF16)
    qkv_ref[:, SB_WIDTH:] = y[:, C_K:C_V + SB_WIDTH].astype(BF16)
    return y[:, C_K:C_K + SB_WIDTH], y[:, C_V:C_V + SB_WIDTH]


def _inproj_body(x_ref, w_ref, k_in, v_in, o_ref, qkv_ref, k_ref, v_ref):
    del k_in, v_in
    k_ref[...], v_ref[...] = _inproj_tail(x_ref[...], w_ref, o_ref, qkv_ref)


def _ln_inproj_body(x_ref, g_ref, b_ref, w_ref, xn_ref, o_ref, qkv_ref, k_ref, v_ref):
    xn = _layer_norm(x_ref[...], g_ref[...], b_ref[...])
    xn_ref[...] = xn
    k, v = _inproj_tail(xn, w_ref, o_ref, qkv_ref)
    k_ref[0], v_ref[0] = k, v
    k_ref[1:] = jnp.zeros_like(k_ref[1:])
    v_ref[1:] = jnp.zeros_like(v_ref[1:])


def in_proj(x, w, tm, layer, ln=None, kv=None):
    t, k = x.shape
    n = w.shape[1]
    rows = lambda c: pl.BlockSpec((tm, c), lambda i: (i, 0))
    const = lambda r, c: pl.BlockSpec((r, c), lambda i: (0, 0))
    stack = jax.ShapeDtypeStruct((DEPTH, t, SB_WIDTH), F32)
    stack_spec = pl.BlockSpec((None, tm, SB_WIDTH), lambda i: (layer, i, 0))
    outs = [(rows(n), jax.ShapeDtypeStruct((t, n), F32)), (rows(3 * SB_WIDTH), jax.ShapeDtypeStruct((t, 3 * SB_WIDTH), BF16)),
            (stack_spec, stack), (stack_spec, stack)]
    if ln is None:
        any_spec = pl.BlockSpec(memory_space=pl.ANY)
        body, ins, args = _inproj_body, [rows(k), const(k, n), any_spec, any_spec], (x, w) + tuple(kv)
        aliases = {2: 2, 3: 3}
    else:
        assert kv is None and layer == 0
        body, ins, args = _ln_inproj_body, [rows(k), const(1, k), const(1, k), const(k, n)], (x, ln[0][None], ln[1][None], w)
        all_layers = pl.BlockSpec((DEPTH, tm, SB_WIDTH), lambda i: (0, i, 0))
        outs = [(rows(k), jax.ShapeDtypeStruct((t, k), F32))] + outs[:2] + [(all_layers, stack), (all_layers, stack)]
        aliases = {}
    return pl.pallas_call(
        body, grid=(t // tm,), in_specs=ins, out_specs=[o[0] for o in outs], out_shape=[o[1] for o in outs],
        input_output_aliases=aliases, compiler_params=_cp("parallel"), name="in_proj")(*args)


def _outproj_body(yl_ref, ys_ref, yd_ref, x_ref, w_ref, g_ref, b_ref, o_ref):
    mixed = jnp.concatenate([yl_ref[...], ys_ref[...], yd_ref[...]], axis=-1).astype(BF16)
    mix = jnp.dot(mixed, w_ref[...], preferred_element_type=F32)
    o_ref[...] = _layer_norm(DN_ALPHA * x_ref[...] + mix, g_ref[...], b_ref[...])


def outproj_ln1(y_lru, y_sb, y_ssd, x, w, g, b, tm):
    t, d = x.shape
    rows = lambda c: pl.BlockSpec((tm, c), lambda i: (i, 0))
    vec = pl.BlockSpec((1, d), lambda i: (0, 0))
    return pl.pallas_call(
        _outproj_body, grid=(t // tm,),
        in_specs=[rows(LRU_WIDTH), rows(SB_WIDTH), rows(SSD_D_INNER), rows(d),
                  pl.BlockSpec((d, d), lambda i: (0, 0)), vec, vec],
        out_specs=rows(d), out_shape=jax.ShapeDtypeStruct((t, d), F32),
        compiler_params=_cp("parallel"), name="outproj_ln1")(y_lru, y_sb, y_ssd, x, w, g.reshape(1, d), b.reshape(1, d))


def _ffn_body(x_ref, wg_ref, wu_ref, wd_ref, p_ref, pg_ref, pp_ref, g_ref, b_ref, o_ref, acc_s):
    f = pl.program_id(1)

    @pl.when(f == 0)
    def _():
        acc_s[...] = jnp.zeros_like(acc_s)

    xb = x_ref[...].astype(BF16)
    g = jnp.dot(xb, wg_ref[...], preferred_element_type=F32)
    u = jnp.dot(xb, wu_ref[...], preferred_element_type=F32)
    h = (_silu(g) * u).astype(BF16)
    acc_s[...] += jnp.dot(h, wd_ref[...], preferred_element_type=F32)

    @pl.when(f == pl.num_programs(1) - 1)
    def _():
        o_ref[...] = _ple_ln2(x_ref[...], acc_s[...], p_ref[...], pg_ref[...], pp_ref[...], g_ref[...], b_ref[...])


def ffn_dense(x, wg, wu, wd, epilogue, tm, tf):
    t, d = x.shape
    ff = wg.shape[1]
    return pl.pallas_call(
        _ffn_body, grid=(t // tm, ff // tf),
        in_specs=[pl.BlockSpec((tm, d), lambda i, f: (i, 0)),
                  pl.BlockSpec((d, tf), lambda i, f: (0, f)),
                  pl.BlockSpec((d, tf), lambda i, f: (0, f)),
                  pl.BlockSpec((tf, d), lambda i, f: (f, 0))] + _epilogue_specs(tm, d, lambda i, f: (i, 0)),
        out_specs=pl.BlockSpec((tm, d), lambda i, f: (i, 0)),
        out_shape=jax.ShapeDtypeStruct((t, d), F32),
        scratch_shapes=[pltpu.VMEM((tm, d), F32)],
        compiler_params=_cp("parallel", "arbitrary"), name="ffn_dense")(x, wg, wu, wd, *epilogue)


def _router_body(x_ref, w_ref, o_ref, sel_ref, cnt_ref):
    logits = jnp.dot(x_ref[...], w_ref[...], preferred_element_type=F32, precision=lax.Precision.HIGHEST)
    lane = _col_iota(logits.shape).astype(F32)
    logits = jnp.where(lane < N_EXPERTS, logits, -jnp.inf)
    m1 = jnp.max(logits, axis=-1, keepdims=True)
    i1 = jnp.min(jnp.where(logits == m1, lane, float(LANES)), axis=-1, keepdims=True)
    rest = jnp.where(lane == i1, -jnp.inf, logits)
    m2 = jnp.max(rest, axis=-1, keepdims=True)
    i2 = jnp.min(jnp.where(rest == m2, lane, float(LANES)), axis=-1, keepdims=True)
    e2 = jnp.exp(m2 - m1)
    w1 = 1.0 / (1.0 + e2)
    w2 = e2 * w1
    o_ref[...] = jnp.where(lane == i1, w1, 0.0) + jnp.where(lane == i2, w2, 0.0)
    tm = logits.shape[0]
    hit1 = jnp.where(lane == i1, 1.0, 0.0)
    hit2 = jnp.where(lane == i2, 1.0, 0.0)
    member = hit1 + hit2
    before = jnp.where(_row_iota((tm, tm)) > _col_iota((tm, tm)), 1.0, 0.0).astype(BF16)
    earlier = jnp.dot(before, member.astype(BF16), preferred_element_type=F32)
    r1 = jnp.sum(hit1 * earlier, axis=-1, keepdims=True)
    r2 = jnp.sum(hit2 * earlier, axis=-1, keepdims=True)
    cnt_ref[...] = jnp.sum(member, axis=0, keepdims=True)
    sel_ref[...] = (jnp.where(lane == 0, i1, 0.0) + jnp.where(lane == 1, i2, 0.0)
                    + jnp.where(lane == 2, w1, 0.0) + jnp.where(lane == 3, w2, 0.0)
                    + jnp.where(lane == 4, r1, 0.0) + jnp.where(lane == 5, r2, 0.0))


def router_gates(x, w_router, tm):
    t, d = x.shape
    wpad = jnp.pad(w_router, ((0, 0), (0, LANES - N_EXPERTS)))
    out = pl.BlockSpec((tm, LANES), lambda i: (i, 0))
    return pl.pallas_call(
        _router_body, grid=(t // tm,),
        in_specs=[pl.BlockSpec((tm, d), lambda i: (i, 0)), pl.BlockSpec((d, LANES), lambda i: (0, 0))],
        out_specs=[out, out, pl.BlockSpec((None, 1, LANES), lambda i: (i, 0, 0))],
        out_shape=[jax.ShapeDtypeStruct((t, LANES), F32)] * 2 + [jax.ShapeDtypeStruct((t // tm, 1, LANES), F32)],
        compiler_params=_cp("parallel"), name="router")(x, wpad)


def _moe_body(x_ref, gates_ref, wg_ref, wu_ref, wd_ref, o_ref, acc_s):
    e = pl.program_id(1)
    f = pl.program_id(2)

    @pl.when(jnp.logical_and(e == 0, f == 0))
    def _():
        acc_s[...] = jnp.zeros_like(acc_s)

    xb = x_ref[...].astype(BF16)
    g = jnp.dot(xb, wg_ref[...], preferred_element_type=F32)
    u = jnp.dot(xb, wu_ref[...], preferred_element_type=F32)
    gates = gates_ref[...]
    gate = jnp.sum(jnp.where(_col_iota(gates.shape) == e, gates, 0.0), axis=-1, keepdims=True)
    h = (_silu(g) * u * gate).astype(BF16)
    acc_s[...] += jnp.dot(h, wd_ref[...], preferred_element_type=F32)

    @pl.when(jnp.logical_and(e == pl.num_programs(1) - 1, f == pl.num_programs(2) - 1))
    def _():
        o_ref[...] = acc_s[...]


def moe_dense(x, gates, wg, wu, wd, tm, tf):
    t, d = x.shape
    ne, _, ff = wg.shape
    return pl.pallas_call(
        _moe_body, grid=(t // tm, ne, ff // tf),
        in_specs=[pl.BlockSpec((tm, d), lambda i, e, f: (i, 0)),
                  pl.BlockSpec((tm, LANES), lambda i, e, f: (i, 0)),
                  pl.BlockSpec((None, d, tf), lambda i, e, f: (e, 0, f)),
                  pl.BlockSpec((None, d, tf), lambda i, e, f: (e, 0, f)),
                  pl.BlockSpec((None, tf, d), lambda i, e, f: (e, f, 0))],
        out_specs=pl.BlockSpec((tm, d), lambda i, e, f: (i, 0)),
        out_shape=jax.ShapeDtypeStruct((t, d), F32),
        scratch_shapes=[pltpu.VMEM((tm, d), F32)],
        compiler_params=_cp("parallel", "arbitrary", "arbitrary"), name="moe_dense")(x, gates, wg, wu, wd)


def _dispatch_body(dest_ref, x_ref, zeros_ref, xs_ref, sem, *, tm):
    del zeros_ref

    def row_copy(r, s):
        return pltpu.make_async_copy(x_ref.at[pl.ds(r, 1)], xs_ref.at[pl.ds(dest_ref[2 * r + s], 1)], sem)

    def issue(r, carry):
        row_copy(r, 0).start()
        row_copy(r, 1).start()
        return carry

    def drain(r, carry):
        row_copy(r, 0).wait()
        row_copy(r, 1).wait()
        return carry

    lax.fori_loop(0, tm, issue, 0, unroll=8)
    lax.fori_loop(0, tm, drain, 0, unroll=8)


def moe_dispatch(x, dest, n_rows, tm):
    t, d = x.shape
    return pl.pallas_call(
        functools.partial(_dispatch_body, tm=tm), grid=(t // tm,),
        in_specs=[pl.BlockSpec((2 * tm,), lambda i: (i,), memory_space=pltpu.SMEM),
                  pl.BlockSpec((tm, d), lambda i: (i, 0)),
                  pl.BlockSpec(memory_space=pl.ANY)],
        out_specs=pl.BlockSpec(memory_space=pl.ANY),
        out_shape=jax.ShapeDtypeStruct((n_rows, d), F32),
        scratch_shapes=[pltpu.SemaphoreType.DMA(())],
        input_output_aliases={2: 0},
        compiler_params=_cp("arbitrary"), name="moe_dispatch")(dest, x, jnp.zeros((n_rows, d), F32))


def _moe_group_body(te_ref, nu_ref, x_ref, wg_ref, wu_ref, wd_ref, o_ref, acc_s):
    del te_ref
    i = pl.program_id(0)
    f = pl.program_id(1)
    last = f == pl.num_programs(1) - 1

    @pl.when(i < nu_ref[0])
    def _():
        xb = x_ref[...].astype(BF16)
        g = jnp.dot(xb, wg_ref[...], preferred_element_type=F32)
        u = jnp.dot(xb, wu_ref[...], preferred_element_type=F32)
        y = jnp.dot((_silu(g) * u).astype(BF16), wd_ref[...], preferred_element_type=F32)

        @pl.when(f == 0)
        def _():
            acc_s[...] = y

        @pl.when(f > 0)
        def _():
            acc_s[...] += y

        @pl.when(last)
        def _():
            o_ref[...] = acc_s[...]

    @pl.when(jnp.logical_and(i >= nu_ref[0], last))
    def _():
        o_ref[...] = jnp.zeros_like(o_ref)


def moe_grouped(xs, tile_expert, n_used, wg, wu, wd, tm, tf):
    r, d = xs.shape
    ff = wg.shape[2]
    grid_spec = pltpu.PrefetchScalarGridSpec(
        num_scalar_prefetch=2, grid=(r // tm, ff // tf),
        in_specs=[pl.BlockSpec((tm, d), lambda i, f, te, nu: (i, 0)),
                  pl.BlockSpec((None, d, tf), lambda i, f, te, nu: (te[i], 0, f)),
                  pl.BlockSpec((None, d, tf), lambda i, f, te, nu: (te[i], 0, f)),
                  pl.BlockSpec((None, tf, d), lambda i, f, te, nu: (te[i], f, 0))],
        out_specs=pl.BlockSpec((tm, d), lambda i, f, te, nu: (i, 0)),
        scratch_shapes=[pltpu.VMEM((tm, d), F32)])
    return pl.pallas_call(
        _moe_group_body, grid_spec=grid_spec, out_shape=jax.ShapeDtypeStruct((r, d), F32),
        compiler_params=_cp("arbitrary", "arbitrary"), name="moe_grouped")(tile_expert, n_used, xs, wg, wu, wd)


def _combine_body(dest_ref, sel_ref, ys_ref, x_ref, p_ref, pg_ref, pp_ref, g_ref, b_ref, o_ref, buf0, buf1, sem,
                  *, tm):
    def row_copy(r, s):
        buf = buf0 if s == 0 else buf1
        return pltpu.make_async_copy(ys_ref.at[pl.ds(dest_ref[2 * r + s], 1)], buf.at[pl.ds(r, 1)], sem)

    def issue(r, carry):
        row_copy(r, 0).start()
        row_copy(r, 1).start()
        return carry

    def drain(r, carry):
        row_copy(r, 0).wait()
        row_copy(r, 1).wait()
        return carry

    lax.fori_loop(0, tm, issue, 0, unroll=8)
    lax.fori_loop(0, tm, drain, 0, unroll=8)
    sel = sel_ref[...]
    f = sel[:, 2:3] * buf0[...] + sel[:, 3:4] * buf1[...]
    o_ref[...] = _ple_ln2(x_ref[...], f, p_ref[...], pg_ref[...], pp_ref[...], g_ref[...], b_ref[...])


def moe_combine(ys, dest, sel, x1, tm, epilogue):
    t, d = x1.shape
    return pl.pallas_call(
        functools.partial(_combine_body, tm=tm), grid=(t // tm,),
        in_specs=[pl.BlockSpec((2 * tm,), lambda i: (i,), memory_space=pltpu.SMEM),
                  pl.BlockSpec((tm, LANES), lambda i: (i, 0)),
                  pl.BlockSpec(memory_space=pl.ANY),
                  pl.BlockSpec((tm, d), lambda i: (i, 0))] + _epilogue_specs(tm, d, lambda i: (i, 0)),
        out_specs=pl.BlockSpec((tm, d), lambda i: (i, 0)),
        out_shape=jax.ShapeDtypeStruct((t, d), F32),
        scratch_shapes=[pltpu.VMEM((tm, d), F32), pltpu.VMEM((tm, d), F32), pltpu.SemaphoreType.DMA(())],
        compiler_params=_cp("arbitrary"), name="moe_combine")(dest, sel, ys, x1, *epilogue)


def moe_sparse(x, sel, tile_counts, wg, wu, wd, tm, tf, epilogue):
    t, _ = x.shape
    ne = wg.shape[0]
    cnt = tile_counts[:, 0, :ne].astype(jnp.int32)
    before_tile = jnp.cumsum(cnt, axis=0) - cnt
    counts = jnp.sum(cnt, axis=0)
    padded = ((counts + tm - 1) // tm) * tm
    ends = jnp.cumsum(padded)
    base = jnp.repeat((ends - padded)[None, :] + before_tile, tm, axis=0)
    experts = jnp.arange(ne, dtype=jnp.int32)[None, :]

    def dest_of(slot):
        hit = sel[:, slot:slot + 1].astype(jnp.int32) == experts
        return jnp.sum(jnp.where(hit, base, 0), axis=1) + sel[:, 4 + slot].astype(jnp.int32)

    dest = jnp.stack([dest_of(0), dest_of(1)], axis=1).reshape(-1)
    n_tiles = (2 * t) // tm + ne
    tile_start = jnp.arange(n_tiles, dtype=jnp.int32) * tm
    tile_expert = jnp.minimum(jnp.sum((tile_start[:, None] >= ends[None, :]).astype(jnp.int32), axis=1), ne - 1)
    n_used = (ends[-1:] // tm).astype(jnp.int32)
    xs = moe_dispatch(x, dest, n_tiles * tm, tm)
    ys = moe_grouped(xs, tile_expert.astype(jnp.int32), n_used, wg, wu, wd, tm, tf)
    return moe_combine(ys, dest, sel, x, tm, epilogue)


def _ple_ln2(x1, f, p, wg, wp, g, b):
    gate = _sigmoid(jnp.dot(x1.astype(BF16), wg, preferred_element_type=F32))
    proj = jnp.dot(p.astype(BF16), wp, preferred_element_type=F32)
    return _layer_norm(DN_ALPHA * x1 + f + gate * proj, g, b)


def _ple_ln2_body(x_ref, f_ref, p_ref, wg_ref, wp_ref, g_ref, b_ref, o_ref):
    o_ref[...] = _ple_ln2(x_ref[...], f_ref[...], p_ref[...], wg_ref[...], wp_ref[...], g_ref[...], b_ref[...])


def _epilogue_specs(tm, d, idx):
    const = lambda *_: (0, 0)
    return [pl.BlockSpec((tm, PLE_DIM), idx), pl.BlockSpec((d, d), const), pl.BlockSpec((PLE_DIM, d), const),
            pl.BlockSpec((1, d), const), pl.BlockSpec((1, d), const)]


def ple_ln2(x1, f, epilogue, tm):
    t, d = x1.shape
    rows = lambda c: pl.BlockSpec((tm, c), lambda i: (i, 0))
    return pl.pallas_call(
        _ple_ln2_body, grid=(t // tm,),
        in_specs=[rows(d), rows(d)] + _epilogue_specs(tm, d, lambda i: (i, 0)),
        out_specs=rows(d), out_shape=jax.ShapeDtypeStruct((t, d), F32),
        compiler_params=_cp("parallel"), name="ple_ln2")(x1, f, *epilogue)


def _lru_body(x_ref, g_ref, prev0_ref, h0_ref, cw_ref, cb_ref, wg_ref, bg_ref, lam_ref,
              y_ref, ht_ref, prev_s, h_s, *, tl, first_is_pos0, valid_len):
    i = pl.program_id(1)

    @pl.when(i == 0)
    def _():
        prev_s[...] = prev0_ref[...]
        h_s[...] = h0_ref[...]

    u = x_ref[...]
    xc = _causal_conv(u, prev_s[...], cw_ref[...], cb_ref[...])
    prev_s[...] = u[tl - SUBLANES:tl]

    gates = jnp.dot(xc.astype(BF16), wg_ref[...], preferred_element_type=F32) + bg_ref[...]
    r = _sigmoid(gates[:, :LRU_WIDTH])
    ig = _sigmoid(gates[:, LRU_WIDTH:])
    a = jnp.exp(-LRU_C * r * _softplus(-lam_ref[...]))
    mult = jnp.sqrt(1.0 - a * a)
    row = i * tl + _row_iota(u.shape)
    if first_is_pos0:
        mult = jnp.where(row == 0, 1.0, mult)
    b = mult * ig * xc
    if valid_len is not None:
        valid = row < valid_len
        a = jnp.where(valid, a, 1.0)
        b = jnp.where(valid, b, 0.0)

    d = 1
    while d < tl:
        b = b + a * _shift_rows_fill(b, d, 0.0)
        a = a * _shift_rows_fill(a, d, 1.0)
        d *= 2
    hs = b + a * h_s[...]
    h_last = hs[tl - 1:tl]
    h_s[...] = h_last
    ht_ref[...] = h_last
    y_ref[...] = hs * _gelu_tanh(g_ref[...])


def lru_mixer(proj, nb, seq, tl, prev0, h0, cw, cb, wgate, bgate, lam, first_is_pos0, valid_len):
    nl = seq // tl
    w = LRU_WIDTH
    vec = lambda r, c: pl.BlockSpec((r, c), lambda b, i: (0, 0))
    body = functools.partial(_lru_body, tl=tl, first_is_pos0=first_is_pos0, valid_len=valid_len)
    return pl.pallas_call(
        body, grid=(nb, nl),
        in_specs=[pl.BlockSpec((tl, w), lambda b, i: (b * nl + i, C_XL // w)),
                  pl.BlockSpec((tl, w), lambda b, i: (b * nl + i, C_GL // w)),
                  pl.BlockSpec((None, SUBLANES, w), lambda b, i: (b, 0, 0)),
                  pl.BlockSpec((None, 1, w), lambda b, i: (b, 0, 0)),
                  vec(SUBLANES, w), vec(1, w), vec(w, 2 * w), vec(1, 2 * w), vec(1, w)],
        out_specs=[pl.BlockSpec((tl, w), lambda b, i: (b * nl + i, 0)),
                   pl.BlockSpec((None, 1, w), lambda b, i: (b, 0, 0))],
        out_shape=[jax.ShapeDtypeStruct((nb * seq, w), F32), jax.ShapeDtypeStruct((nb, 1, w), F32)],
        scratch_shapes=[pltpu.VMEM((SUBLANES, w), F32), pltpu.VMEM((1, w), F32)],
        compiler_params=_cp("parallel", "arbitrary"), name="lru")(
            proj, proj, prev0, h0, cw, cb, wgate, bgate, lam)


def _ssd_body(xbc_ref, z_ref, dt_ref, prev0_ref, s0_ref, cw_ref, cb_ref, dtb_ref, a_ref, d_ref, nw_ref,
              y_ref, st_ref, prev_s, s_s, *, rows, valid_len):
    i = pl.program_id(1)
    q = SSD_CHUNK
    hp = SSD_HEAD_DIM
    ns = SSD_D_STATE

    @pl.when(i == 0)
    def _():
        prev_s[...] = prev0_ref[...]
        s_s[...] = s0_ref[...]

    u = xbc_ref[...]
    xbc = _silu(_causal_conv(u, prev_s[...], cw_ref[...], cb_ref[...]))
    prev_s[...] = u[rows - SUBLANES:rows]
    dt = _softplus(dt_ref[...] + dtb_ref[...])
    if valid_len is not None:
        dt = jnp.where(i * rows + _row_iota(dt.shape) < valid_len, dt, 0.0)
    if rows < q:
        xbc = jnp.concatenate([xbc, jnp.zeros((q - rows, xbc.shape[1]), F32)], axis=0)
        dt = jnp.concatenate([dt, jnp.zeros((q - rows, dt.shape[1]), F32)], axis=0)
    xs = xbc[:, :SSD_D_INNER]
    bm = xbc[:, SSD_D_INNER:SSD_D_INNER + SSD_GROUPS * ns]
    cm = xbc[:, SSD_D_INNER + SSD_GROUPS * ns:]

    a_cs = dt * a_ref[...]
    d = 1
    while d < q:
        a_cs = a_cs + _shift_rows_fill(a_cs, d, 0.0)
        d *= 2
    a_cs_t = a_cs.T
    tri = _row_iota((q, q)) >= _col_iota((q, q))

    rep = SSD_HEADS // SSD_GROUPS
    nt = (((1,), (1,)), ((), ()))
    tn = (((0,), (0,)), ((), ()))
    cb_g = [lax.dot_general(cm[:, g * ns:(g + 1) * ns], bm[:, g * ns:(g + 1) * ns], nt,
                            preferred_element_type=F32) for g in range(SSD_GROUPS)]
    ys = []
    for h in range(SSD_HEADS):
        g = h // rep
        col = a_cs[:, h:h + 1]
        seg = col - a_cs_t[h:h + 1, :]
        lmat = jnp.exp(jnp.where(tri, seg, -jnp.inf))
        xs_h = xs[:, h * hp:(h + 1) * hp]
        xd = xs_h * dt[:, h:h + 1]
        c_g = cm[:, g * ns:(g + 1) * ns]
        b_g = bm[:, g * ns:(g + 1) * ns]
        s_prev = s_s[h]
        y_diag = jnp.dot(cb_g[g] * lmat, xd, preferred_element_type=F32)
        y_off = jnp.exp(col) * lax.dot_general(c_g, s_prev, nt, preferred_element_type=F32)
        a_last = a_cs[q - 1:q, h:h + 1]
        s_new = jnp.exp(a_last) * s_prev + lax.dot_general(xd * jnp.exp(a_last - col), b_g, tn,
                                                            preferred_element_type=F32)
        s_s[h] = s_new
        st_ref[h] = s_new
        ys.append(y_diag + y_off)
    y = jnp.concatenate(ys, axis=-1) + d_ref[...] * xs
    if rows < q:
        y = y[:rows]
    y = y * _silu(z_ref[...])
    y_ref[...] = y * lax.rsqrt(jnp.mean(y * y, axis=-1, keepdims=True) + RMS_EPS) * nw_ref[...]


def ssd_mixer(proj, nb, seq, rows, prev0, s0, cw, cb, dtb, a_neg, d_skip, norm_w, valid_len):
    nl = seq // rows
    cdim = SSD_CONV_DIM
    di = SSD_D_INNER
    vec = lambda r, c: pl.BlockSpec((r, c), lambda b, i: (0, 0))
    st = pl.BlockSpec((None, SSD_HEADS, SSD_HEAD_DIM, SSD_D_STATE), lambda b, i: (b, 0, 0, 0))
    body = functools.partial(_ssd_body, rows=rows, valid_len=valid_len)
    return pl.pallas_call(
        body, grid=(nb, nl),
        in_specs=[pl.BlockSpec((rows, cdim), lambda b, i: (b * nl + i, C_XBC // cdim)),
                  pl.BlockSpec((rows, di), lambda b, i: (b * nl + i, C_Z // di)),
                  pl.BlockSpec((rows, LANES), lambda b, i: (b * nl + i, C_DT // LANES)),
                  pl.BlockSpec((None, SUBLANES, cdim), lambda b, i: (b, 0, 0)),
                  st, vec(SUBLANES, cdim), vec(1, cdim), vec(1, LANES), vec(1, LANES), vec(1, di), vec(1, di)],
        out_specs=[pl.BlockSpec((rows, di), lambda b, i: (b * nl + i, 0)), st],
        out_shape=[jax.ShapeDtypeStruct((nb * seq, di), F32),
                   jax.ShapeDtypeStruct((nb, SSD_HEADS, SSD_HEAD_DIM, SSD_D_STATE), F32)],
        scratch_shapes=[pltpu.VMEM((SUBLANES, cdim), F32),
                        pltpu.VMEM((SSD_HEADS, SSD_HEAD_DIM, SSD_D_STATE), F32)],
        compiler_params=_cp("parallel", "arbitrary"), name="ssd")(
            proj, proj, proj, prev0, s0, cw, cb, dtb, a_neg, d_skip, norm_w)


def _neg_strict_upper(n):
    return jnp.where(_row_iota((n, n)) > _col_iota((n, n)), -1.0, 0.0).astype(BF16)


def _neg_abs(x):
    return pltpu.bitcast(pltpu.bitcast(x, jnp.uint32) | jnp.uint32(0x80000000), F32)


def _sb_prompt_body(bias_ref, q_ref, k_ref, v_ref, o_ref, c_s, rs_s, p_s, qm_s, *, tq):
    i = pl.program_id(1)
    hd = SB_HEAD_DIM
    neg_upper = _neg_strict_upper(tq)
    diag_mask = _col_iota((tq, tq)) < _row_iota((tq, tq))
    nt = (((1,), (1,)), ((), ()))
    H = SB_HEADS
    pw = 2 * hd
    low_half = _col_iota((tq, pw)) < hd
    lane_x = _col_iota((tq, pw))
    for h in range(H):
        qp = q_ref[:, (h // 2) * pw:(h // 2 + 1) * pw]
        qm = jnp.where(low_half if h % 2 == 0 else jnp.logical_not(low_half), qp, jnp.zeros_like(qp))
        b0 = jnp.full((tq, pw), bias_ref[h] * LOG2E, F32)
        b_hi = b0.astype(BF16).astype(F32)
        b_mid = (b0 - b_hi).astype(BF16).astype(F32)
        b_lo = b0 - b_hi - b_mid
        ext = jnp.where(lane_x == 0, b_hi, jnp.where(lane_x == 1, b_mid, jnp.where(lane_x == 2, b_lo, 0.0)))
        qm_s[h] = jnp.concatenate([qm, ext.astype(BF16)], axis=1)
    ones_ext = jnp.ones((tq, pw), BF16)

    def weights(kb, mask):
        r0 = pl.multiple_of(kb * tq, tq)
        zs = []
        for h in range(H):
            pcols = slice((h // 2) * pw, (h // 2 + 1) * pw)
            k_ext = jnp.concatenate([k_ref[pl.ds(r0, tq), pcols], ones_ext], axis=1)
            zs.append(lax.dot_general(qm_s[h], k_ext, nt, preferred_element_type=F32))
        sps, lbs = [], []
        for h in range(H):
            z = zs[h]
            sp = jnp.maximum(z, 0.0) + jnp.log2(1.0 + jnp.exp2(_neg_abs(z)))
            lbs.append(z - sp)
            if mask is not None:
                sp = jnp.where(mask, sp, 0.0)
            sps.append(sp)
        for h in range(H):
            sp = sps[h]
            later = jnp.dot(sp.astype(BF16), neg_upper, preferred_element_type=F32)
            p = jnp.exp2(lbs[h] + later)
            if mask is not None:
                p = jnp.where(mask, p, 0.0)
            p_s[h] = p.astype(BF16)
            rs_s[h] = jnp.sum(sp, axis=-1, keepdims=True)

    def accumulate(kb, first):
        r0 = pl.multiple_of(kb * tq, tq)
        for hp in range(H // 2):
            pcols = slice(hp * pw, (hp + 1) * pw)
            vp = v_ref[pl.ds(r0, tq), pcols]
            pv = []
            for h in (2 * hp, 2 * hp + 1):
                x = jnp.dot(p_s[h], vp, preferred_element_type=F32)
                if first:
                    c_s[h] = rs_s[h]
                else:
                    c = c_s[h]
                    x = jnp.exp2(-c) * x
                    c_s[h] = c + rs_s[h]
                pv.append(x)
            both = jnp.where(low_half, pv[0], pv[1])
            if first:
                o_ref[:, pcols] = both
            else:
                o_ref[:, pcols] += both

    weights(i, diag_mask)

    @pl.when(i > 0)
    def _():
        accumulate(i, True)
        weights(i - 1, None)

        def body(kk, carry):
            accumulate(i - kk + 1, False)
            weights(i - kk, None)
            return carry

        lax.fori_loop(2, i + 1, body, 0)
        accumulate(0, False)

    @pl.when(i == 0)
    def _():
        accumulate(0, True)


def sb_prompt(qkv, bias, nb, seq):
    tq = SB_TILE
    nq = seq // tq
    w = SB_WIDTH
    return pl.pallas_call(
        functools.partial(_sb_prompt_body, tq=tq), grid=(nb, nq),
        in_specs=[pl.BlockSpec(memory_space=pltpu.SMEM),
                  pl.BlockSpec((tq, w), lambda b, i: (b * nq + i, 0)),
                  pl.BlockSpec((seq, w), lambda b, i: (b, 1)),
                  pl.BlockSpec((seq, w), lambda b, i: (b, 2))],
        out_specs=pl.BlockSpec((tq, w), lambda b, i: (b * nq + i, 0)),
        out_shape=jax.ShapeDtypeStruct((nb * seq, w), F32),
        scratch_shapes=[pltpu.VMEM((SB_HEADS, tq, 1), F32), pltpu.VMEM((SB_HEADS, tq, 1), F32),
                        pltpu.VMEM((SB_HEADS, tq, tq), BF16),
                        pltpu.VMEM((SB_HEADS, tq, 4 * SB_HEAD_DIM), BF16)],
        compiler_params=_cp("parallel", "arbitrary"), name="sb_prompt")(bias, qkv, qkv, qkv)


def _sb_blocks_heads(q3, kts, vts, bias, neg_upper, mask):
    nh, nr, _ = q3.shape
    zs = [lax.dot_general(q3, kt, (((2,), (1,)), ((0,), (0,))), preferred_element_type=F32) + bias
          for kt in kts]
    sps, lbs = [], []
    for z in zs:
        sp = _softplus(z)
        lbs.append(z - sp)
        sps.append(sp if mask is None else jnp.where(mask, sp, 0.0))
    ps = []
    for sp, lb in zip(sps, lbs):
        nk = sp.shape[2]
        sp2 = sp.reshape(nh * nr, nk)
        hi = sp2.astype(BF16)
        lo = (sp2 - hi.astype(F32)).astype(BF16)
        later = (jnp.dot(hi, neg_upper, preferred_element_type=F32)
                 + jnp.dot(lo, neg_upper, preferred_element_type=F32)).reshape(nh, nr, nk)
        p = jnp.exp(lb + later)
        ps.append((p if mask is None else jnp.where(mask, p, 0.0)).astype(BF16))
    pvs = [lax.dot_general(p, vt, (((2,), (2,)), ((0,), (0,))), preferred_element_type=F32)
           for p, vt in zip(ps, vts)]
    return pvs, [jnp.sum(sp, axis=-1, keepdims=True) for sp in sps]


def _sb_sample_body(pt_ref, q_ref, bias_ref, kn_ref, vn_ref, ck_ref, cv_ref, o_ref,
                    kbuf, vbuf, sems, c_s, acc_s, *, npg, layer):
    b = pl.program_id(0)
    j = pl.program_id(1)
    nsteps = pl.num_programs(1)
    npages = nsteps * npg
    step = b * nsteps + j
    slot = lax.rem(step, 2)
    ps = PAGE_SIZE
    q3 = q_ref[...]
    bias = bias_ref[...]
    neg_upper = _neg_strict_upper(ps)

    def page_copies(seq, jj, sl, n):
        page = pt_ref[seq, npages - 1 - (jj * npg + n)]
        return (pltpu.make_async_copy(ck_ref.at[layer, page], kbuf.at[sl, n], sems.at[sl]),
                pltpu.make_async_copy(cv_ref.at[layer, page], vbuf.at[sl, n], sems.at[sl]))

    def fetch(seq, jj, sl):
        for n in range(npg):
            for cp in page_copies(seq, jj, sl, n):
                cp.start()

    @pl.when(step == 0)
    def _():
        fetch(b, j, slot)

    @pl.when(step + 1 < pl.num_programs(0) * nsteps)
    def _():
        last = j == nsteps - 1
        fetch(jnp.where(last, b + 1, b), jnp.where(last, 0, j + 1), 1 - slot)

    @pl.when(j == 0)
    def _():
        shape = q3.shape[:2] + (ps,)
        mask = lax.broadcasted_iota(jnp.int32, shape, 2) < lax.broadcasted_iota(jnp.int32, shape, 1)
        pvs, rss = _sb_blocks_heads(q3, [kn_ref[...].astype(BF16)], [vn_ref[...].astype(BF16)], bias,
                                    neg_upper, mask)
        acc_s[...] = pvs[0]
        c_s[...] = rss[0]

    for n in range(npg):
        for cp in page_copies(b, j, slot, n):
            cp.wait()
    pvs, rss = _sb_blocks_heads(q3, [kbuf[slot, n].astype(BF16) for n in range(npg)],
                                [vbuf[slot, n].astype(BF16) for n in range(npg)], bias, neg_upper, None)
    c = c_s[...]
    acc = acc_s[...]
    for pv, rs in zip(pvs, rss):
        acc = acc + jnp.exp(-c) * pv
        c = c + rs
    c_s[...] = c
    acc_s[...] = acc

    @pl.when(j == nsteps - 1)
    def _():
        o_ref[...] = acc


def sb_sample(q3, bias, k_new_t, v_new_t, cache_k_t, cache_v_t, page_table, layer):
    nseq, npages = page_table.shape
    npg = PAGES_PER_STEP
    nsteps = npages // npg
    _, nh, nr, hd = q3.shape
    per_seq = lambda r, c: pl.BlockSpec((None, nh, r, c), lambda b, j, pt: (b, 0, 0, 0))
    any_spec = pl.BlockSpec(memory_space=pl.ANY)
    grid_spec = pltpu.PrefetchScalarGridSpec(
        num_scalar_prefetch=1, grid=(nseq, nsteps),
        in_specs=[per_seq(nr, hd), pl.BlockSpec((nh, 1, 1), lambda b, j, pt: (0, 0, 0)),
                  per_seq(hd, PAGE_SIZE), per_seq(hd, PAGE_SIZE), any_spec, any_spec],
        out_specs=per_seq(nr, hd),
        scratch_shapes=[pltpu.VMEM((2, npg, nh, hd, PAGE_SIZE), F32), pltpu.VMEM((2, npg, nh, hd, PAGE_SIZE), F32),
                        pltpu.SemaphoreType.DMA((2,)),
                        pltpu.VMEM((nh, nr, 1), F32), pltpu.VMEM((nh, nr, hd), F32)])
    return pl.pallas_call(
        functools.partial(_sb_sample_body, npg=npg, layer=layer), grid_spec=grid_spec,
        out_shape=jax.ShapeDtypeStruct((nseq, nh, nr, hd), F32),
        compiler_params=_cp("arbitrary", "arbitrary"), name="sb_sample")(
            page_table, q3, bias, k_new_t, v_new_t, cache_k_t, cache_v_t)


def _pad_prev(buf):
    return jnp.pad(buf, ((0, 0), (SUBLANES - (CONV_W - 1), 0), (0, 0)))


def _prep_weights(W):
    P = {}
    w_in = W['w_in']
    o = [0, 256, 512, 1024, 1536, 2048, 2304, 2816, 3328]
    P['w_in'] = jnp.concatenate(
        [w_in[:, :, o[0]:o[5]], w_in[:, :, o[6]:o[7]], w_in[:, :, o[5]:o[6]],
         jnp.pad(w_in[:, :, o[7]:o[8]], ((0, 0), (0, 0), (0, C_END - C_DT - SSD_HEADS)))], axis=-1).astype(BF16)
    eye = jnp.eye(LRU_BLOCKS, dtype=F32)
    bd = lambda w: jnp.einsum('lgij,gh->lgihj', w, eye).reshape(DEPTH, LRU_WIDTH, LRU_WIDTH)
    P['lru_wgate'] = jnp.concatenate([bd(W['lru_wa']), bd(W['lru_wx'])], axis=-1).astype(BF16)
    P['lru_bgate'] = jnp.concatenate([W['lru_ba'], W['lru_bx']], axis=-1)[:, None, :]
    pad_cw = lambda w: jnp.pad(w, ((0, 0), (0, SUBLANES - CONV_W), (0, 0)))
    P['lru_cw'] = pad_cw(W['lru_conv_w'])
    P['ssd_cw'] = pad_cw(W['ssd_conv_w'])
    lane_pad = lambda v: jnp.pad(v, ((0, 0), (0, LANES - SSD_HEADS)))[:, None, :]
    P['ssd_dtb'] = lane_pad(W['ssd_dt_bias'])
    P['ssd_a'] = lane_pad(-jnp.exp(W['ssd_a_log']))
    P['ssd_d'] = jnp.repeat(W['ssd_d'], SSD_HEAD_DIM, axis=-1)[:, None, :]
    for name in ('w_out', 'ffn_w_gate', 'ffn_w_up', 'ffn_w_down', 'moe_w_gate', 'moe_w_up', 'moe_w_down',
                 'ple_w_gate', 'ple_w_proj'):
        P[name] = W[name].astype(BF16)
    return P


def _run_group(x, p, nb, seq, valid_len, at_pos0, lru_h0, lru_buf0, ssd_s0, ssd_buf0, sb_fn, W, P):
    t = nb * seq
    tm = min(512, t)
    tl = min(256, seq)
    rows = min(SSD_CHUNK, seq)
    vl = None if valid_len == seq else valid_len
    states = []
    for li in range(DEPTH):
        if li == 0:
            x, proj, qkv16, k_all, v_all = in_proj(x, P['w_in'][li], min(256, t), li, ln=(W['ln_in_g'], W['ln_in_b']))
        else:
            proj, qkv16, k_all, v_all = in_proj(x, P['w_in'][li], min(256, t), li, kv=(k_all, v_all))
        y_lru, lru_ht = lru_mixer(proj, nb, seq, tl, _pad_prev(lru_buf0[li]), lru_h0[li][:, None, :],
                                  P['lru_cw'][li], W['lru_conv_b'][li][None], P['lru_wgate'][li],
                                  P['lru_bgate'][li], W['lru_lambda'][li][None], at_pos0, vl)
        y_ssd, ssd_st = ssd_mixer(proj, nb, seq, rows, _pad_prev(ssd_buf0[li]), ssd_s0[li],
                                  P['ssd_cw'][li], W['ssd_conv_b'][li][None], P['ssd_dtb'][li], P['ssd_a'][li],
                                  P['ssd_d'][li], W['ssd_norm_w'][li][None], vl)
        y_sb = sb_fn(li, proj, qkv16)
        x1 = outproj_ln1(y_lru, y_sb, y_ssd, x, P['w_out'][li], W['ln1_g'][li], W['ln1_b'][li], tm)
        epilogue = (p[li], P['ple_w_gate'][li], P['ple_w_proj'][li], W['ln2_g'][li][None], W['ln2_b'][li][None])
        j = li // 2
        if li % 2 == 0:
            x = ffn_dense(x1, P['ffn_w_gate'][j], P['ffn_w_up'][j], P['ffn_w_down'][j], epilogue, tm, 1408)
        else:
            gates, sel, tile_counts = router_gates(x1, W['moe_router'][j], tm)
            if t >= MOE_SPARSE_MIN_TOKENS:
                x = moe_sparse(x1, sel, tile_counts, P['moe_w_gate'][j], P['moe_w_up'][j], P['moe_w_down'][j],
                               tm, 1792, epilogue)
            else:
                f = moe_dense(x1, gates, P['moe_w_gate'][j], P['moe_w_up'][j], P['moe_w_down'][j], tm, 1792)
                x = ple_ln2(x1, f, epilogue, tm)
        p3 = proj.reshape(nb, seq, C_END)
        tail = slice(valid_len - (CONV_W - 1), valid_len)
        states.append((lru_ht[:, 0], p3[:, tail, C_XL:C_XL + LRU_WIDTH], ssd_st,
                       p3[:, tail, C_XBC:C_XBC + SSD_CONV_DIM]))
    heads = lambda a: a.reshape(DEPTH, nb, seq, SB_HEADS, SB_HEAD_DIM)[:, :, :valid_len]
    stacked = (heads(k_all), heads(v_all)) + tuple(jnp.stack([st[n] for st in states]) for n in range(4))
    return x, stacked


def kernel(x_prompt, x_sample, p_prompt, p_sample, cache_k, cache_v, page_table, state_lru_h, state_lru_conv, state_ssd, state_ssd_conv, ln_in_g, ln_in_b, w_in, lru_conv_w, lru_conv_b, lru_wa, lru_ba, lru_wx, lru_bx, lru_lambda, sb_bias, ssd_conv_w, ssd_conv_b, ssd_dt_bias, ssd_a_log, ssd_d, ssd_norm_w, w_out, ln1_g, ln1_b, ln2_g, ln2_b, ffn_w_gate, ffn_w_up, ffn_w_down, moe_router, moe_w_gate, moe_w_up, moe_w_down, ple_w_gate, ple_w_proj):
    W = dict(ln_in_g=ln_in_g, ln_in_b=ln_in_b, w_in=w_in, lru_conv_w=lru_conv_w, lru_conv_b=lru_conv_b,
             lru_wa=lru_wa, lru_ba=lru_ba, lru_wx=lru_wx, lru_bx=lru_bx, lru_lambda=lru_lambda,
             sb_bias=sb_bias, ssd_conv_w=ssd_conv_w, ssd_conv_b=ssd_conv_b, ssd_dt_bias=ssd_dt_bias,
             ssd_a_log=ssd_a_log, ssd_d=ssd_d, ssd_norm_w=ssd_norm_w, w_out=w_out, ln1_g=ln1_g,
             ln1_b=ln1_b, ln2_g=ln2_g, ln2_b=ln2_b, ffn_w_gate=ffn_w_gate, ffn_w_up=ffn_w_up,
             ffn_w_down=ffn_w_down, moe_router=moe_router, moe_w_gate=moe_w_gate, moe_w_up=moe_w_up,
             moe_w_down=moe_w_down, ple_w_gate=ple_w_gate, ple_w_proj=ple_w_proj)
    P = _prep_weights(W)
    d = D_MODEL

    bp, sp, _ = x_prompt.shape
    zeros = lambda *s: jnp.zeros((DEPTH, bp) + s, F32)

    def prompt_sb(li, proj, qkv16):
        del proj
        return sb_prompt(qkv16, sb_bias[li], bp, sp)

    y_prompt, st_p = _run_group(
        x_prompt.reshape(bp * sp, d), p_prompt.reshape(DEPTH, bp * sp, PLE_DIM), bp, sp, sp, True,
        zeros(LRU_WIDTH), zeros(CONV_W - 1, LRU_WIDTH), zeros(SSD_HEADS, SSD_HEAD_DIM, SSD_D_STATE),
        zeros(CONV_W - 1, SSD_CONV_DIM), prompt_sb, W, P)

    bs, ts, _ = x_sample.shape
    rs = SAMPLE_ROWS
    pad_rows = lambda a, ax: jnp.pad(a, [(0, rs - ts) if n == ax else (0, 0) for n in range(a.ndim)])
    scale = 1.0 / math.sqrt(SB_HEAD_DIM)
    cache_k_t = jnp.transpose(cache_k, (0, 1, 3, 4, 2))
    cache_v_t = jnp.transpose(cache_v, (0, 1, 3, 4, 2))
    row_valid = (jnp.arange(rs) < ts)[None, None, :, None]

    def sample_sb(li, proj, qkv16):
        del qkv16
        p3 = proj.reshape(bs, rs, C_END)
        heads = lambda c: p3[:, :, c:c + SB_WIDTH].reshape(bs, rs, SB_HEADS, SB_HEAD_DIM)
        q3 = jnp.where(row_valid, jnp.transpose(heads(C_Q), (0, 2, 1, 3)) * scale, 0.0).astype(BF16)
        new_t = lambda c: jnp.pad(jnp.transpose(heads(c), (0, 2, 3, 1)), ((0, 0),) * 3 + ((0, PAGE_SIZE - rs),))
        out = sb_sample(q3, sb_bias[li][:, None, None], new_t(C_K), new_t(C_V), cache_k_t, cache_v_t,
                        page_table, li)
        return jnp.transpose(out, (0, 2, 1, 3)).reshape(bs * rs, SB_WIDTH)

    y_sample, st_s = _run_group(
        pad_rows(x_sample, 1).reshape(bs * rs, d), pad_rows(p_sample, 2).reshape(DEPTH, bs * rs, PLE_DIM),
        bs, rs, ts, False, state_lru_h, state_lru_conv, state_ssd, state_ssd_conv, sample_sb, W, P)
    y_sample = y_sample.reshape(bs, rs, d)[:, :ts]

    return (y_prompt.reshape(bp, sp, d), y_sample) + st_p + st_s
```

```python
import functools
import math

import jax
import jax.numpy as jnp
from jax import lax
from jax.experimental import pallas as pl
from jax.experimental.pallas import tpu as pltpu

F32 = jnp.float32
BF16 = jnp.bfloat16

D_MODEL = 1024
DEPTH = 2
PAGE_SIZE = 128
LRU_WIDTH = 256
LRU_BLOCKS = 4
LRU_C = 8.0
SB_HEADS = 8
SB_HEAD_DIM = 64
SB_WIDTH = SB_HEADS * SB_HEAD_DIM
SSD_HEADS = 4
SSD_HEAD_DIM = 64
SSD_D_INNER = SSD_HEADS * SSD_HEAD_DIM
SSD_GROUPS = 2
SSD_D_STATE = 64
SSD_CONV_DIM = SSD_D_INNER + 2 * SSD_GROUPS * SSD_D_STATE
SSD_CHUNK = 128
CONV_W = 4
N_EXPERTS = 8
PLE_DIM = 256
DN_ALPHA = (2 * DEPTH) ** 0.25
LN_EPS = 1e-5
RMS_EPS = 1e-6
LOG2E = math.log2(math.e)

LANES = 128
SUBLANES = 8
VMEM_LIMIT = 56 * 1024 * 1024

C_XL, C_GL, C_Q, C_K, C_V, C_XBC, C_Z, C_DT, C_END = 0, 256, 512, 1024, 1536, 2048, 2560, 2816, 3072
SAMPLE_ROWS = 8
SB_TILE = 256
PAGES_PER_STEP = 16
MOE_SPARSE_MIN_TOKENS = 4096


def _cp(*sem):
    return pltpu.CompilerParams(dimension_semantics=sem, vmem_limit_bytes=VMEM_LIMIT)


def _layer_norm(x, g, b):
    mu = jnp.mean(x, axis=-1, keepdims=True)
    xc = x - mu
    var = jnp.mean(xc * xc, axis=-1, keepdims=True)
    return xc * lax.rsqrt(var + LN_EPS) * g + b


def _softplus(x):
    return jnp.maximum(x, 0.0) + jnp.log(1.0 + jnp.exp(-jnp.abs(x)))


def _sigmoid(x):
    return 1.0 / (1.0 + jnp.exp(-x))


def _silu(x):
    return x * _sigmoid(x)


def _gelu_tanh(x):
    return 0.5 * x * (1.0 + jnp.tanh(math.sqrt(2.0 / math.pi) * (x + 0.044715 * (x * x * x))))


def _row_iota(shape):
    return lax.broadcasted_iota(jnp.int32, shape, 0)


def _col_iota(shape):
    return lax.broadcasted_iota(jnp.int32, shape, 1)


def _shift_rows_prev(u, prev8, k):
    r = pltpu.roll(u, k, 0)
    head = jnp.where(_row_iota(prev8.shape) < k, pltpu.roll(prev8, k, 0), r[:SUBLANES])
    if u.shape[0] == SUBLANES:
        return head
    return jnp.concatenate([head, r[SUBLANES:]], axis=0)


def _causal_conv(u, prev8, cw, cb):
    out = cb + cw[3:4] * u
    for k in range(1, CONV_W):
        out = out + cw[CONV_W - 1 - k:CONV_W - k] * _shift_rows_prev(u, prev8, k)
    return out


def _shift_rows_fill(x, d, fill):
    return jnp.where(_row_iota(x.shape) >= d, pltpu.roll(x, d, 0), fill)


def _inproj_tail(x, w_ref, o_ref, qkv_ref):
    y = jnp.dot(x.astype(BF16), w_ref[...], preferred_element_type=F32)
    o_ref[...] = y
    qkv_ref[:, :SB_WIDTH] = (y[:, C_Q:C_Q + SB_WIDTH] * (LOG2E / math.sqrt(SB_HEAD_DIM))).astype(BF16)
    qkv_ref[:, SB_WIDTH:] = y[:, C_K:C_V + SB_WIDTH].astype(BF16)
    return y[:, C_K:C_K + SB_WIDTH], y[:, C_V:C_V + SB_WIDTH]


def _inproj_body(x_ref, w_ref, o_ref, qkv_ref):
    _inproj_tail(x_ref[...], w_ref, o_ref, qkv_ref)


def _ln_inproj_body(x_ref, g_ref, b_ref, w_ref, xn_ref, o_ref, qkv_ref):
    xn = _layer_norm(x_ref[...], g_ref[...], b_ref[...])
    xn_ref[...] = xn
    _inproj_tail(xn, w_ref, o_ref, qkv_ref)


def _inproj_kv_body(x_ref, w_ref, k_in, v_in, o_ref, qkv_ref, kt_ref, vt_ref):
    del k_in, v_in
    k, v = _inproj_tail(x_ref[...], w_ref, o_ref, qkv_ref)
    kt_ref[...], vt_ref[...] = k.T, v.T


def _ln_inproj_kv_body(x_ref, g_ref, b_ref, w_ref, xn_ref, o_ref, qkv_ref, kt_ref, vt_ref):
    xn = _layer_norm(x_ref[...], g_ref[...], b_ref[...])
    xn_ref[...] = xn
    k, v = _inproj_tail(xn, w_ref, o_ref, qkv_ref)
    kt_ref[0], vt_ref[0] = k.T, v.T
    kt_ref[1:] = jnp.zeros_like(kt_ref[1:])
    vt_ref[1:] = jnp.zeros_like(vt_ref[1:])


def in_proj(x, w, tm, layer, ln=None, kv=None, seq=None):
    t, k = x.shape
    n = w.shape[1]
    rows = lambda c: pl.BlockSpec((tm, c), lambda i: (i, 0))
    const = lambda r, c: pl.BlockSpec((r, c), lambda i: (0, 0))
    ins, args = [rows(k), const(k, n)], (x, w)
    outs = [(rows(n), jax.ShapeDtypeStruct((t, n), F32)), (rows(3 * SB_WIDTH), jax.ShapeDtypeStruct((t, 3 * SB_WIDTH), BF16))]
    aliases = {}
    if ln is not None:
        assert layer == 0 and kv is None
        ins, args = [rows(k), const(1, k), const(1, k), const(k, n)], (x, ln[0][None], ln[1][None], w)
        outs = [(rows(k), jax.ShapeDtypeStruct((t, k), F32))] + outs
    if seq is None:
        body = _inproj_body if ln is None else _ln_inproj_body
    else:
        per_seq = seq // tm
        stack = jax.ShapeDtypeStruct((DEPTH, t // seq, SB_WIDTH, seq), F32)
        if ln is None:
            body = _inproj_kv_body
            spec = pl.BlockSpec((None, None, SB_WIDTH, tm), lambda i: (layer, i // per_seq, 0, i % per_seq))
            any_spec = pl.BlockSpec(memory_space=pl.ANY)
            ins, args = ins + [any_spec, any_spec], args + tuple(kv)
            aliases = {len(ins) - 2: len(outs), len(ins) - 1: len(outs) + 1}
        else:
            body = _ln_inproj_kv_body
            spec = pl.BlockSpec((DEPTH, None, SB_WIDTH, tm), lambda i: (0, i // per_seq, 0, i % per_seq))
        outs = outs + [(spec, stack), (spec, stack)]
    return pl.pallas_call(
        body, grid=(t // tm,), in_specs=ins, out_specs=[o[0] for o in outs], out_shape=[o[1] for o in outs],
        input_output_aliases=aliases, compiler_params=_cp("parallel"), name="in_proj")(*args)


def _outproj_body(yl_ref, ys_ref, yd_ref, x_ref, w_ref, g_ref, b_ref, o_ref):
    mixed = jnp.concatenate([yl_ref[...], ys_ref[...], yd_ref[...]], axis=-1).astype(BF16)
    mix = jnp.dot(mixed, w_ref[...], preferred_element_type=F32)
    o_ref[...] = _layer_norm(DN_ALPHA * x_ref[...] + mix, g_ref[...], b_ref[...])


def outproj_ln1(y_lru, y_sb, y_ssd, x, w, g, b, tm):
    t, d = x.shape
    rows = lambda c: pl.BlockSpec((tm, c), lambda i: (i, 0))
    vec = pl.BlockSpec((1, d), lambda i: (0, 0))
    return pl.pallas_call(
        _outproj_body, grid=(t // tm,),
        in_specs=[rows(LRU_WIDTH), rows(SB_WIDTH), rows(SSD_D_INNER), rows(d),
                  pl.BlockSpec((d, d), lambda i: (0, 0)), vec, vec],
        out_specs=rows(d), out_shape=jax.ShapeDtypeStruct((t, d), F32),
        compiler_params=_cp("parallel"), name="outproj_ln1")(y_lru, y_sb, y_ssd, x, w, g.reshape(1, d), b.reshape(1, d))


def _ffn_body(x_ref, wg_ref, wu_ref, wd_ref, p_ref, pg_ref, pp_ref, g_ref, b_ref, o_ref, acc_s):
    f = pl.program_id(1)

    @pl.when(f == 0)
    def _():
        acc_s[...] = jnp.zeros_like(acc_s)

    xb = x_ref[...].astype(BF16)
    g = jnp.dot(xb, wg_ref[...], preferred_element_type=F32)
    u = jnp.dot(xb, wu_ref[...], preferred_element_type=F32)
    h = (_silu(g) * u).astype(BF16)
    acc_s[...] += jnp.dot(h, wd_ref[...], preferred_element_type=F32)

    @pl.when(f == pl.num_programs(1) - 1)
    def _():
        o_ref[...] = _ple_ln2(x_ref[...], acc_s[...], p_ref[...], pg_ref[...], pp_ref[...], g_ref[...], b_ref[...])


def ffn_dense(x, wg, wu, wd, epilogue, tm, tf):
    t, d = x.shape
    ff = wg.shape[1]
    return pl.pallas_call(
        _ffn_body, grid=(t // tm, ff // tf),
        in_specs=[pl.BlockSpec((tm, d), lambda i, f: (i, 0)),
                  pl.BlockSpec((d, tf), lambda i, f: (0, f)),
                  pl.BlockSpec((d, tf), lambda i, f: (0, f)),
                  pl.BlockSpec((tf, d), lambda i, f: (f, 0))] + _epilogue_specs(tm, d, lambda i, f: (i, 0)),
        out_specs=pl.BlockSpec((tm, d), lambda i, f: (i, 0)),
        out_shape=jax.ShapeDtypeStruct((t, d), F32),
        scratch_shapes=[pltpu.VMEM((tm, d), F32)],
        compiler_params=_cp("parallel", "arbitrary"), name="ffn_dense")(x, wg, wu, wd, *epilogue)


def _router_body(x_ref, w_ref, o_ref, sel_ref, cnt_ref):
    logits = jnp.dot(x_ref[...], w_ref[...], preferred_element_type=F32, precision=lax.Precision.HIGHEST)
    lane = _col_iota(logits.shape).astype(F32)
    logits = jnp.where(lane < N_EXPERTS, logits, -jnp.inf)
    m1 = jnp.max(logits, axis=-1, keepdims=True)
    i1 = jnp.min(jnp.where(logits == m1, lane, float(LANES)), axis=-1, keepdims=True)
    rest = jnp.where(lane == i1, -jnp.inf, logits)
    m2 = jnp.max(rest, axis=-1, keepdims=True)
    i2 = jnp.min(jnp.where(rest == m2, lane, float(LANES)), axis=-1, keepdims=True)
    e2 = jnp.exp(m2 - m1)
    w1 = 1.0 / (1.0 + e2)
    w2 = e2 * w1
    o_ref[...] = jnp.where(lane == i1, w1, 0.0) + jnp.where(lane == i2, w2, 0.0)
    tm = logits.shape[0]
    hit1 = jnp.where(lane == i1, 1.0, 0.0)
    hit2 = jnp.where(lane == i2, 1.0, 0.0)
    member = hit1 + hit2
    before = jnp.where(_row_iota((tm, tm)) > _col_iota((tm, tm)), 1.0, 0.0).astype(BF16)
    earlier = jnp.dot(before, member.astype(BF16), preferred_element_type=F32)
    r1 = jnp.sum(hit1 * earlier, axis=-1, keepdims=True)
    r2 = jnp.sum(hit2 * earlier, axis=-1, keepdims=True)
    cnt_ref[...] = jnp.sum(member, axis=0, keepdims=True)
    sel_ref[...] = (jnp.where(lane == 0, i1, 0.0) + jnp.where(lane == 1, i2, 0.0)
                    + jnp.where(lane == 2, w1, 0.0) + jnp.where(lane == 3, w2, 0.0)
                    + jnp.where(lane == 4, r1, 0.0) + jnp.where(lane == 5, r2, 0.0))


def router_gates(x, w_router, tm):
    t, d = x.shape
    wpad = jnp.pad(w_router, ((0, 0), (0, LANES - N_EXPERTS)))
    out = pl.BlockSpec((tm, LANES), lambda i: (i, 0))
    return pl.pallas_call(
        _router_body, grid=(t // tm,),
        in_specs=[pl.BlockSpec((tm, d), lambda i: (i, 0)), pl.BlockSpec((d, LANES), lambda i: (0, 0))],
        out_specs=[out, out, pl.BlockSpec((None, 1, LANES), lambda i: (i, 0, 0))],
        out_shape=[jax.ShapeDtypeStruct((t, LANES), F32)] * 2 + [jax.ShapeDtypeStruct((t // tm, 1, LANES), F32)],
        compiler_params=_cp("parallel"), name="router")(x, wpad)


def _moe_body(x_ref, gates_ref, wg_ref, wu_ref, wd_ref, o_ref, acc_s):
    e = pl.program_id(1)
    f = pl.program_id(2)

    @pl.when(jnp.logical_and(e == 0, f == 0))
    def _():
        acc_s[...] = jnp.zeros_like(acc_s)

    xb = x_ref[...].astype(BF16)
    g = jnp.dot(xb, wg_ref[...], preferred_element_type=F32)
    u = jnp.dot(xb, wu_ref[...], preferred_element_type=F32)
    gates = gates_ref[...]
    gate = jnp.sum(jnp.where(_col_iota(gates.shape) == e, gates, 0.0), axis=-1, keepdims=True)
    h = (_silu(g) * u * gate).astype(BF16)
    acc_s[...] += jnp.dot(h, wd_ref[...], preferred_element_type=F32)

    @pl.when(jnp.logical_and(e == pl.num_programs(1) - 1, f == pl.num_programs(2) - 1))
    def _():
        o_ref[...] = acc_s[...]


def moe_dense(x, gates, wg, wu, wd, tm, tf):
    t, d = x.shape
    ne, _, ff = wg.shape
    return pl.pallas_call(
        _moe_body, grid=(t // tm, ne, ff // tf),
        in_specs=[pl.BlockSpec((tm, d), lambda i, e, f: (i, 0)),
                  pl.BlockSpec((tm, LANES), lambda i, e, f: (i, 0)),
                  pl.BlockSpec((None, d, tf), lambda i, e, f: (e, 0, f)),
                  pl.BlockSpec((None, d, tf), lambda i, e, f: (e, 0, f)),
                  pl.BlockSpec((None, tf, d), lambda i, e, f: (e, f, 0))],
        out_specs=pl.BlockSpec((tm, d), lambda i, e, f: (i, 0)),
        out_shape=jax.ShapeDtypeStruct((t, d), F32),
        scratch_shapes=[pltpu.VMEM((tm, d), F32)],
        compiler_params=_cp("parallel", "arbitrary", "arbitrary"), name="moe_dense")(x, gates, wg, wu, wd)


def _dispatch_body(dest_ref, x_ref, zeros_ref, xs_ref, sem, *, tm):
    del zeros_ref

    def row_copy(r, s):
        return pltpu.make_async_copy(x_ref.at[pl.ds(r, 1)], xs_ref.at[pl.ds(dest_ref[2 * r + s], 1)], sem)

    def issue(r, carry):
        row_copy(r, 0).start()
        row_copy(r, 1).start()
        return carry

    def drain(r, carry):
        row_copy(r, 0).wait()
        row_copy(r, 1).wait()
        return carry

    lax.fori_loop(0, tm, issue, 0, unroll=8)
    lax.fori_loop(0, tm, drain, 0, unroll=8)


def moe_dispatch(x, dest, n_rows, tm):
    t, d = x.shape
    return pl.pallas_call(
        functools.partial(_dispatch_body, tm=tm), grid=(t // tm,),
        in_specs=[pl.BlockSpec((2 * tm,), lambda i: (i,), memory_space=pltpu.SMEM),
                  pl.BlockSpec((tm, d), lambda i: (i, 0)),
                  pl.BlockSpec(memory_space=pl.ANY)],
        out_specs=pl.BlockSpec(memory_space=pl.ANY),
        out_shape=jax.ShapeDtypeStruct((n_rows, d), F32),
        scratch_shapes=[pltpu.SemaphoreType.DMA(())],
        input_output_aliases={2: 0},
        compiler_params=_cp("arbitrary"), name="moe_dispatch")(dest, x, jnp.zeros((n_rows, d), F32))


def _moe_group_body(te_ref, nu_ref, x_ref, wg_ref, wu_ref, wd_ref, o_ref, acc_s):
    del te_ref
    i = pl.program_id(0)
    f = pl.program_id(1)
    last = f == pl.num_programs(1) - 1

    @pl.when(i < nu_ref[0])
    def _():
        xb = x_ref[...].astype(BF16)
        g = jnp.dot(xb, wg_ref[...], preferred_element_type=F32)
        u = jnp.dot(xb, wu_ref[...], preferred_element_type=F32)
        y = jnp.dot((_silu(g) * u).astype(BF16), wd_ref[...], preferred_element_type=F32)

        @pl.when(f == 0)
        def _():
            acc_s[...] = y

        @pl.when(f > 0)
        def _():
            acc_s[...] += y

        @pl.when(last)
        def _():
            o_ref[...] = acc_s[...]

    @pl.when(jnp.logical_and(i >= nu_ref[0], last))
    def _():
        o_ref[...] = jnp.zeros_like(o_ref)


def moe_grouped(xs, tile_expert, n_used, wg, wu, wd, tm, tf):
    r, d = xs.shape
    ff = wg.shape[2]
    grid_spec = pltpu.PrefetchScalarGridSpec(
        num_scalar_prefetch=2, grid=(r // tm, ff // tf),
        in_specs=[pl.BlockSpec((tm, d), lambda i, f, te, nu: (i, 0)),
                  pl.BlockSpec((None, d, tf), lambda i, f, te, nu: (te[i], 0, f)),
                  pl.BlockSpec((None, d, tf), lambda i, f, te, nu: (te[i], 0, f)),
                  pl.BlockSpec((None, tf, d), lambda i, f, te, nu: (te[i], f, 0))],
        out_specs=pl.BlockSpec((tm, d), lambda i, f, te, nu: (i, 0)),
        scratch_shapes=[pltpu.VMEM((tm, d), F32)])
    return pl.pallas_call(
        _moe_group_body, grid_spec=grid_spec, out_shape=jax.ShapeDtypeStruct((r, d), F32),
        compiler_params=_cp("arbitrary", "arbitrary"), name="moe_grouped")(tile_expert, n_used, xs, wg, wu, wd)


def _combine_body(dest_ref, sel_ref, ys_ref, x_ref, p_ref, pg_ref, pp_ref, g_ref, b_ref, o_ref, buf0, buf1, sem,
                  *, tm):
    def row_copy(r, s):
        buf = buf0 if s == 0 else buf1
        return pltpu.make_async_copy(ys_ref.at[pl.ds(dest_ref[2 * r + s], 1)], buf.at[pl.ds(r, 1)], sem)

    def issue(r, carry):
        row_copy(r, 0).start()
        row_copy(r, 1).start()
        return carry

    def drain(r, carry):
        row_copy(r, 0).wait()
        row_copy(r, 1).wait()
        return carry

    lax.fori_loop(0, tm, issue, 0, unroll=8)
    lax.fori_loop(0, tm, drain, 0, unroll=8)
    sel = sel_ref[...]
    f = sel[:, 2:3] * buf0[...] + sel[:, 3:4] * buf1[...]
    o_ref[...] = _ple_ln2(x_ref[...], f, p_ref[...], pg_ref[...], pp_ref[...], g_ref[...], b_ref[...])


def moe_combine(ys, dest, sel, x1, tm, epilogue):
    t, d = x1.shape
    return pl.pallas_call(
        functools.partial(_combine_body, tm=tm), grid=(t // tm,),
        in_specs=[pl.BlockSpec((2 * tm,), lambda i: (i,), memory_space=pltpu.SMEM),
                  pl.BlockSpec((tm, LANES), lambda i: (i, 0)),
                  pl.BlockSpec(memory_space=pl.ANY),
                  pl.BlockSpec((tm, d), lambda i: (i, 0))] + _epilogue_specs(tm, d, lambda i: (i, 0)),
        out_specs=pl.BlockSpec((tm, d), lambda i: (i, 0)),
        out_shape=jax.ShapeDtypeStruct((t, d), F32),
        scratch_shapes=[pltpu.VMEM((tm, d), F32), pltpu.VMEM((tm, d), F32), pltpu.SemaphoreType.DMA(())],
        compiler_params=_cp("arbitrary"), name="moe_combine")(dest, sel, ys, x1, *epilogue)


def moe_sparse(x, sel, tile_counts, wg, wu, wd, tm, tf, epilogue):
    t, _ = x.shape
    ne = wg.shape[0]
    cnt = tile_counts[:, 0, :ne].astype(jnp.int32)
    before_tile = jnp.cumsum(cnt, axis=0) - cnt
    counts = jnp.sum(cnt, axis=0)
    padded = ((counts + tm - 1) // tm) * tm
    ends = jnp.cumsum(padded)
    base = jnp.repeat((ends - padded)[None, :] + before_tile, tm, axis=0)
    experts = jnp.arange(ne, dtype=jnp.int32)[None, :]

    def dest_of(slot):
        hit = sel[:, slot:slot + 1].astype(jnp.int32) == experts
        return jnp.sum(jnp.where(hit, base, 0), axis=1) + sel[:, 4 + slot].astype(jnp.int32)

    dest = jnp.stack([dest_of(0), dest_of(1)], axis=1).reshape(-1)
    n_tiles = (2 * t) // tm + ne
    tile_start = jnp.arange(n_tiles, dtype=jnp.int32) * tm
    tile_expert = jnp.minimum(jnp.sum((tile_start[:, None] >= ends[None, :]).astype(jnp.int32), axis=1), ne - 1)
    n_used = (ends[-1:] // tm).astype(jnp.int32)
    xs = moe_dispatch(x, dest, n_tiles * tm, tm)
    ys = moe_grouped(xs, tile_expert.astype(jnp.int32), n_used, wg, wu, wd, tm, tf)
    return moe_combine(ys, dest, sel, x, tm, epilogue)


def _ple_ln2(x1, f, p, wg, wp, g, b):
    gate = _sigmoid(jnp.dot(x1.astype(BF16), wg, preferred_element_type=F32))
    proj = jnp.dot(p.astype(BF16), wp, preferred_element_type=F32)
    return _layer_norm(DN_ALPHA * x1 + f + gate * proj, g, b)


def _ple_ln2_body(x_ref, f_ref, p_ref, wg_ref, wp_ref, g_ref, b_ref, o_ref):
    o_ref[...] = _ple_ln2(x_ref[...], f_ref[...], p_ref[...], wg_ref[...], wp_ref[...], g_ref[...], b_ref[...])


def _epilogue_specs(tm, d, idx):
    const = lambda *_: (0, 0)
    return [pl.BlockSpec((tm, PLE_DIM), idx), pl.BlockSpec((d, d), const), pl.BlockSpec((PLE_DIM, d), const),
            pl.BlockSpec((1, d), const), pl.BlockSpec((1, d), const)]


def ple_ln2(x1, f, epilogue, tm):
    t, d = x1.shape
    rows = lambda c: pl.BlockSpec((tm, c), lambda i: (i, 0))
    return pl.pallas_call(
        _ple_ln2_body, grid=(t // tm,),
        in_specs=[rows(d), rows(d)] + _epilogue_specs(tm, d, lambda i: (i, 0)),
        out_specs=rows(d), out_shape=jax.ShapeDtypeStruct((t, d), F32),
        compiler_params=_cp("parallel"), name="ple_ln2")(x1, f, *epilogue)


def _lru_body(x_ref, g_ref, prev0_ref, h0_ref, cw_ref, cb_ref, wg_ref, bg_ref, lam_ref,
              y_ref, ht_ref, prev_s, h_s, *, tl, first_is_pos0, valid_len):
    i = pl.program_id(1)

    @pl.when(i == 0)
    def _():
        prev_s[...] = prev0_ref[...]
        h_s[...] = h0_ref[...]

    u = x_ref[...]
    xc = _causal_conv(u, prev_s[...], cw_ref[...], cb_ref[...])
    prev_s[...] = u[tl - SUBLANES:tl]

    gates = jnp.dot(xc.astype(BF16), wg_ref[...], preferred_element_type=F32) + bg_ref[...]
    r = _sigmoid(gates[:, :LRU_WIDTH])
    ig = _sigmoid(gates[:, LRU_WIDTH:])
    a = jnp.exp(-LRU_C * r * _softplus(-lam_ref[...]))
    mult = jnp.sqrt(1.0 - a * a)
    row = i * tl + _row_iota(u.shape)
    if first_is_pos0:
        mult = jnp.where(row == 0, 1.0, mult)
    b = mult * ig * xc
    if valid_len is not None:
        valid = row < valid_len
        a = jnp.where(valid, a, 1.0)
        b = jnp.where(valid, b, 0.0)

    d = 1
    while d < tl:
        b = b + a * _shift_rows_fill(b, d, 0.0)
        a = a * _shift_rows_fill(a, d, 1.0)
        d *= 2
    hs = b + a * h_s[...]
    h_last = hs[tl - 1:tl]
    h_s[...] = h_last
    ht_ref[...] = h_last
    y_ref[...] = hs * _gelu_tanh(g_ref[...])


def lru_mixer(proj, nb, seq, tl, prev0, h0, cw, cb, wgate, bgate, lam, first_is_pos0, valid_len):
    nl = seq // tl
    w = LRU_WIDTH
    vec = lambda r, c: pl.BlockSpec((r, c), lambda b, i: (0, 0))
    body = functools.partial(_lru_body, tl=tl, first_is_pos0=first_is_pos0, valid_len=valid_len)
    return pl.pallas_call(
        body, grid=(nb, nl),
        in_specs=[pl.BlockSpec((tl, w), lambda b, i: (b * nl + i, C_XL // w)),
                  pl.BlockSpec((tl, w), lambda b, i: (b * nl + i, C_GL // w)),
                  pl.BlockSpec((None, SUBLANES, w), lambda b, i: (b, 0, 0)),
                  pl.BlockSpec((None, 1, w), lambda b, i: (b, 0, 0)),
                  vec(SUBLANES, w), vec(1, w), vec(w, 2 * w), vec(1, 2 * w), vec(1, w)],
        out_specs=[pl.BlockSpec((tl, w), lambda b, i: (b * nl + i, 0)),
                   pl.BlockSpec((None, 1, w), lambda b, i: (b, 0, 0))],
        out_shape=[jax.ShapeDtypeStruct((nb * seq, w), F32), jax.ShapeDtypeStruct((nb, 1, w), F32)],
        scratch_shapes=[pltpu.VMEM((SUBLANES, w), F32), pltpu.VMEM((1, w), F32)],
        compiler_params=_cp("parallel", "arbitrary"), name="lru")(
            proj, proj, prev0, h0, cw, cb, wgate, bgate, lam)


def _ssd_body(xbc_ref, z_ref, dt_ref, prev0_ref, s0_ref, cw_ref, cb_ref, dtb_ref, a_ref, d_ref, nw_ref,
              y_ref, st_ref, prev_s, s_s, *, rows, valid_len):
    i = pl.program_id(1)
    q = SSD_CHUNK
    hp = SSD_HEAD_DIM
    ns = SSD_D_STATE

    @pl.when(i == 0)
    def _():
        prev_s[...] = prev0_ref[...]
        s_s[...] = s0_ref[...]

    u = xbc_ref[...]
    xbc = _silu(_causal_conv(u, prev_s[...], cw_ref[...], cb_ref[...]))
    prev_s[...] = u[rows - SUBLANES:rows]
    dt = _softplus(dt_ref[...] + dtb_ref[...])
    if valid_len is not None:
        dt = jnp.where(i * rows + _row_iota(dt.shape) < valid_len, dt, 0.0)
    if rows < q:
        xbc = jnp.concatenate([xbc, jnp.zeros((q - rows, xbc.shape[1]), F32)], axis=0)
        dt = jnp.concatenate([dt, jnp.zeros((q - rows, dt.shape[1]), F32)], axis=0)
    xs = xbc[:, :SSD_D_INNER]
    bm = xbc[:, SSD_D_INNER:SSD_D_INNER + SSD_GROUPS * ns]
    cm = xbc[:, SSD_D_INNER + SSD_GROUPS * ns:]

    a_cs = dt * a_ref[...]
    d = 1
    while d < q:
        a_cs = a_cs + _shift_rows_fill(a_cs, d, 0.0)
        d *= 2
    a_cs_t = a_cs.T
    tri = _row_iota((q, q)) >= _col_iota((q, q))

    rep = SSD_HEADS // SSD_GROUPS
    nt = (((1,), (1,)), ((), ()))
    tn = (((0,), (0,)), ((), ()))
    cb_g = [lax.dot_general(cm[:, g * ns:(g + 1) * ns], bm[:, g * ns:(g + 1) * ns], nt,
                            preferred_element_type=F32) for g in range(SSD_GROUPS)]
    ys = []
    for h in range(SSD_HEADS):
        g = h // rep
        col = a_cs[:, h:h + 1]
        seg = col - a_cs_t[h:h + 1, :]
        lmat = jnp.exp(jnp.where(tri, seg, -jnp.inf))
        xs_h = xs[:, h * hp:(h + 1) * hp]
        xd = xs_h * dt[:, h:h + 1]
        c_g = cm[:, g * ns:(g + 1) * ns]
        b_g = bm[:, g * ns:(g + 1) * ns]
        s_prev = s_s[h]
        y_diag = jnp.dot(cb_g[g] * lmat, xd, preferred_element_type=F32)
        y_off = jnp.exp(col) * lax.dot_general(c_g, s_prev, nt, preferred_element_type=F32)
        a_last = a_cs[q - 1:q, h:h + 1]
        s_new = jnp.exp(a_last) * s_prev + lax.dot_general(xd * jnp.exp(a_last - col), b_g, tn,
                                                            preferred_element_type=F32)
        s_s[h] = s_new
        st_ref[h] = s_new
        ys.append(y_diag + y_off)
    y = jnp.concatenate(ys, axis=-1) + d_ref[...] * xs
    if rows < q:
        y = y[:rows]
    y = y * _silu(z_ref[...])
    y_ref[...] = y * lax.rsqrt(jnp.mean(y * y, axis=-1, keepdims=True) + RMS_EPS) * nw_ref[...]


def ssd_mixer(proj, nb, seq, rows, prev0, s0, cw, cb, dtb, a_neg, d_skip, norm_w, valid_len):
    nl = seq // rows
    cdim = SSD_CONV_DIM
    di = SSD_D_INNER
    vec = lambda r, c: pl.BlockSpec((r, c), lambda b, i: (0, 0))
    st = pl.BlockSpec((None, SSD_HEADS, SSD_HEAD_DIM, SSD_D_STATE), lambda b, i: (b, 0, 0, 0))
    body = functools.partial(_ssd_body, rows=rows, valid_len=valid_len)
    return pl.pallas_call(
        body, grid=(nb, nl),
        in_specs=[pl.BlockSpec((rows, cdim), lambda b, i: (b * nl + i, C_XBC // cdim)),
                  pl.BlockSpec((rows, di), lambda b, i: (b * nl + i, C_Z // di)),
                  pl.BlockSpec((rows, LANES), lambda b, i: (b * nl + i, C_DT // LANES)),
                  pl.BlockSpec((None, SUBLANES, cdim), lambda b, i: (b, 0, 0)),
                  st, vec(SUBLANES, cdim), vec(1, cdim), vec(1, LANES), vec(1, LANES), vec(1, di), vec(1, di)],
        out_specs=[pl.BlockSpec((rows, di), lambda b, i: (b * nl + i, 0)), st],
        out_shape=[jax.ShapeDtypeStruct((nb * seq, di), F32),
                   jax.ShapeDtypeStruct((nb, SSD_HEADS, SSD_HEAD_DIM, SSD_D_STATE), F32)],
        scratch_shapes=[pltpu.VMEM((SUBLANES, cdim), F32),
                        pltpu.VMEM((SSD_HEADS, SSD_HEAD_DIM, SSD_D_STATE), F32)],
        compiler_params=_cp("parallel", "arbitrary"), name="ssd")(
            proj, proj, proj, prev0, s0, cw, cb, dtb, a_neg, d_skip, norm_w)


def _neg_strict_upper(n):
    return jnp.where(_row_iota((n, n)) > _col_iota((n, n)), -1.0, 0.0).astype(BF16)


def _neg_abs(x):
    return pltpu.bitcast(pltpu.bitcast(x, jnp.uint32) | jnp.uint32(0x80000000), F32)


def _sb_prompt_body(bias_ref, q_ref, k_ref, v_ref, o_ref, c_s, rs_s, p_s, qm_s, *, tq):
    i = pl.program_id(1)
    hd = SB_HEAD_DIM
    neg_upper = _neg_strict_upper(tq)
    diag_mask = _col_iota((tq, tq)) < _row_iota((tq, tq))
    nt = (((1,), (1,)), ((), ()))
    H = SB_HEADS
    pw = 2 * hd
    low_half = _col_iota((tq, pw)) < hd
    lane_x = _col_iota((tq, pw))
    for h in range(H):
        qp = q_ref[:, (h // 2) * pw:(h // 2 + 1) * pw]
        qm = jnp.where(low_half if h % 2 == 0 else jnp.logical_not(low_half), qp, jnp.zeros_like(qp))
        b0 = jnp.full((tq, pw), bias_ref[h] * LOG2E, F32)
        b_hi = b0.astype(BF16).astype(F32)
        b_mid = (b0 - b_hi).astype(BF16).astype(F32)
        b_lo = b0 - b_hi - b_mid
        ext = jnp.where(lane_x == 0, b_hi, jnp.where(lane_x == 1, b_mid, jnp.where(lane_x == 2, b_lo, 0.0)))
        qm_s[h] = jnp.concatenate([qm, ext.astype(BF16)], axis=1)
    ones_ext = jnp.ones((tq, pw), BF16)

    def weights(kb, mask):
        r0 = pl.multiple_of(kb * tq, tq)
        zs = []
        for h in range(H):
            pcols = slice((h // 2) * pw, (h // 2 + 1) * pw)
            k_ext = jnp.concatenate([k_ref[pl.ds(r0, tq), pcols], ones_ext], axis=1)
            zs.append(lax.dot_general(qm_s[h], k_ext, nt, preferred_element_type=F32))
        sps, lbs = [], []
        for h in range(H):
            z = zs[h]
            sp = jnp.maximum(z, 0.0) + jnp.log2(1.0 + jnp.exp2(_neg_abs(z)))
            lbs.append(z - sp)
            if mask is not None:
                sp = jnp.where(mask, sp, 0.0)
            sps.append(sp)
        for h in range(H):
            sp = sps[h]
            later = jnp.dot(sp.astype(BF16), neg_upper, preferred_element_type=F32)
            p = jnp.exp2(lbs[h] + later)
            if mask is not None:
                p = jnp.where(mask, p, 0.0)
            p_s[h] = p.astype(BF16)
            rs_s[h] = jnp.sum(sp, axis=-1, keepdims=True)

    def accumulate(kb, first):
        r0 = pl.multiple_of(kb * tq, tq)
        for hp in range(H // 2):
            pcols = slice(hp * pw, (hp + 1) * pw)
            vp = v_ref[pl.ds(r0, tq), pcols]
            pv = []
            for h in (2 * hp, 2 * hp + 1):
                x = jnp.dot(p_s[h], vp, preferred_element_type=F32)
                if first:
                    c_s[h] = rs_s[h]
                else:
                    c = c_s[h]
                    x = jnp.exp2(-c) * x
                    c_s[h] = c + rs_s[h]
                pv.append(x)
            both = jnp.where(low_half, pv[0], pv[1])
            if first:
                o_ref[:, pcols] = both
            else:
                o_ref[:, pcols] += both

    weights(i, diag_mask)

    @pl.when(i > 0)
    def _():
        accumulate(i, True)
        weights(i - 1, None)

        def body(kk, carry):
            accumulate(i - kk + 1, False)
            weights(i - kk, None)
            return carry

        lax.fori_loop(2, i + 1, body, 0)
        accumulate(0, False)

    @pl.when(i == 0)
    def _():
        accumulate(0, True)


def sb_prompt(qkv, bias, nb, seq):
    tq = SB_TILE
    nq = seq // tq
    w = SB_WIDTH
    return pl.pallas_call(
        functools.partial(_sb_prompt_body, tq=tq), grid=(nb, nq),
        in_specs=[pl.BlockSpec(memory_space=pltpu.SMEM),
                  pl.BlockSpec((tq, w), lambda b, i: (b * nq + i, 0)),
                  pl.BlockSpec((seq, w), lambda b, i: (b, 1)),
                  pl.BlockSpec((seq, w), lambda b, i: (b, 2))],
        out_specs=pl.BlockSpec((tq, w), lambda b, i: (b * nq + i, 0)),
        out_shape=jax.ShapeDtypeStruct((nb * seq, w), F32),
        scratch_shapes=[pltpu.VMEM((SB_HEADS, tq, 1), F32), pltpu.VMEM((SB_HEADS, tq, 1), F32),
                        pltpu.VMEM((SB_HEADS, tq, tq), BF16),
                        pltpu.VMEM((SB_HEADS, tq, 4 * SB_HEAD_DIM), BF16)],
        compiler_params=_cp("parallel", "arbitrary"), name="sb_prompt")(bias, qkv, qkv, qkv)


def _sb_blocks_heads(q3, kts, vts, bias, neg_upper, mask):
    nh, nr, _ = q3.shape
    zs = [lax.dot_general(q3, kt, (((2,), (1,)), ((0,), (0,))), preferred_element_type=F32) + bias
          for kt in kts]
    sps, lbs = [], []
    for z in zs:
        sp = _softplus(z)
        lbs.append(z - sp)
        sps.append(sp if mask is None else jnp.where(mask, sp, 0.0))
    ps = []
    for sp, lb in zip(sps, lbs):
        nk = sp.shape[2]
        sp2 = sp.reshape(nh * nr, nk)
        hi = sp2.astype(BF16)
        lo = (sp2 - hi.astype(F32)).astype(BF16)
        later = (jnp.dot(hi, neg_upper, preferred_element_type=F32)
                 + jnp.dot(lo, neg_upper, preferred_element_type=F32)).reshape(nh, nr, nk)
        p = jnp.exp(lb + later)
        ps.append((p if mask is None else jnp.where(mask, p, 0.0)).astype(BF16))
    pvs = [lax.dot_general(p, vt, (((2,), (2,)), ((0,), (0,))), preferred_element_type=F32)
           for p, vt in zip(ps, vts)]
    return pvs, [jnp.sum(sp, axis=-1, keepdims=True) for sp in sps]


def _sb_sample_body(pt_ref, q_ref, bias_ref, kn_ref, vn_ref, ck_ref, cv_ref, o_ref,
                    kbuf, vbuf, sems, c_s, acc_s, *, npg, layer):
    b = pl.program_id(0)
    j = pl.program_id(1)
    nsteps = pl.num_programs(1)
    npages = nsteps * npg
    step = b * nsteps + j
    slot = lax.rem(step, 2)
    ps = PAGE_SIZE
    q3 = q_ref[...]
    bias = bias_ref[...]
    neg_upper = _neg_strict_upper(ps)

    def page_copies(seq, jj, sl, n):
        page = pt_ref[seq, npages - 1 - (jj * npg + n)]
        return (pltpu.make_async_copy(ck_ref.at[layer, page], kbuf.at[sl, n], sems.at[sl]),
                pltpu.make_async_copy(cv_ref.at[layer, page], vbuf.at[sl, n], sems.at[sl]))

    def fetch(seq, jj, sl):
        for n in range(npg):
            for cp in page_copies(seq, jj, sl, n):
                cp.start()

    @pl.when(step == 0)
    def _():
        fetch(b, j, slot)

    @pl.when(step + 1 < pl.num_programs(0) * nsteps)
    def _():
        last = j == nsteps - 1
        fetch(jnp.where(last, b + 1, b), jnp.where(last, 0, j + 1), 1 - slot)

    @pl.when(j == 0)
    def _():
        shape = q3.shape[:2] + (ps,)
        mask = lax.broadcasted_iota(jnp.int32, shape, 2) < lax.broadcasted_iota(jnp.int32, shape, 1)
        pvs, rss = _sb_blocks_heads(q3, [kn_ref[...].astype(BF16)], [vn_ref[...].astype(BF16)], bias,
                                    neg_upper, mask)
        acc_s[...] = pvs[0]
        c_s[...] = rss[0]

    for n in range(npg):
        for cp in page_copies(b, j, slot, n):
            cp.wait()
    pvs, rss = _sb_blocks_heads(q3, [kbuf[slot, n].astype(BF16) for n in range(npg)],
                                [vbuf[slot, n].astype(BF16) for n in range(npg)], bias, neg_upper, None)
    c = c_s[...]
    acc = acc_s[...]
    for pv, rs in zip(pvs, rss):
        acc = acc + jnp.exp(-c) * pv
        c = c + rs
    c_s[...] = c
    acc_s[...] = acc

    @pl.when(j == nsteps - 1)
    def _():
        o_ref[...] = acc


def sb_sample(q3, bias, k_new_t, v_new_t, cache_k_t, cache_v_t, page_table, layer):
    nseq, npages = page_table.shape
    npg = PAGES_PER_STEP
    nsteps = npages // npg
    _, nh, nr, hd = q3.shape
    per_seq = lambda r, c: pl.BlockSpec((None, nh, r, c), lambda b, j, pt: (b, 0, 0, 0))
    any_spec = pl.BlockSpec(memory_space=pl.ANY)
    grid_spec = pltpu.PrefetchScalarGridSpec(
        num_scalar_prefetch=1, grid=(nseq, nsteps),
        in_specs=[per_seq(nr, hd), pl.BlockSpec((nh, 1, 1), lambda b, j, pt: (0, 0, 0)),
                  per_seq(hd, PAGE_SIZE), per_seq(hd, PAGE_SIZE), any_spec, any_spec],
        out_specs=per_seq(nr, hd),
        scratch_shapes=[pltpu.VMEM((2, npg, nh, hd, PAGE_SIZE), F32), pltpu.VMEM((2, npg, nh, hd, PAGE_SIZE), F32),
                        pltpu.SemaphoreType.DMA((2,)),
                        pltpu.VMEM((nh, nr, 1), F32), pltpu.VMEM((nh, nr, hd), F32)])
    return pl.pallas_call(
        functools.partial(_sb_sample_body, npg=npg, layer=layer), grid_spec=grid_spec,
        out_shape=jax.ShapeDtypeStruct((nseq, nh, nr, hd), F32),
        compiler_params=_cp("arbitrary", "arbitrary"), name="sb_sample")(
            page_table, q3, bias, k_new_t, v_new_t, cache_k_t, cache_v_t)


def _pad_prev(buf):
    return jnp.pad(buf, ((0, 0), (SUBLANES - (CONV_W - 1), 0), (0, 0)))


def _prep_weights(W):
    P = {}
    w_in = W['w_in']
    o = [0, 256, 512, 1024, 1536, 2048, 2304, 2816, 3328]
    P['w_in'] = jnp.concatenate(
        [w_in[:, :, o[0]:o[5]], w_in[:, :, o[6]:o[7]], w_in[:, :, o[5]:o[6]],
         jnp.pad(w_in[:, :, o[7]:o[8]], ((0, 0), (0, 0), (0, C_END - C_DT - SSD_HEADS)))], axis=-1).astype(BF16)
    eye = jnp.eye(LRU_BLOCKS, dtype=F32)
    bd = lambda w: jnp.einsum('lgij,gh->lgihj', w, eye).reshape(DEPTH, LRU_WIDTH, LRU_WIDTH)
    P['lru_wgate'] = jnp.concatenate([bd(W['lru_wa']), bd(W['lru_wx'])], axis=-1).astype(BF16)
    P['lru_bgate'] = jnp.concatenate([W['lru_ba'], W['lru_bx']], axis=-1)[:, None, :]
    pad_cw = lambda w: jnp.pad(w, ((0, 0), (0, SUBLANES - CONV_W), (0, 0)))
    P['lru_cw'] = pad_cw(W['lru_conv_w'])
    P['ssd_cw'] = pad_cw(W['ssd_conv_w'])
    lane_pad = lambda v: jnp.pad(v, ((0, 0), (0, LANES - SSD_HEADS)))[:, None, :]
    P['ssd_dtb'] = lane_pad(W['ssd_dt_bias'])
    P['ssd_a'] = lane_pad(-jnp.exp(W['ssd_a_log']))
    P['ssd_d'] = jnp.repeat(W['ssd_d'], SSD_HEAD_DIM, axis=-1)[:, None, :]
    for name in ('w_out', 'ffn_w_gate', 'ffn_w_up', 'ffn_w_down', 'moe_w_gate', 'moe_w_up', 'moe_w_down',
                 'ple_w_gate', 'ple_w_proj'):
        P[name] = W[name].astype(BF16)
    return P


def _run_group(x, p, nb, seq, valid_len, at_pos0, lru_h0, lru_buf0, ssd_s0, ssd_buf0, sb_fn, W, P):
    t = nb * seq
    tm = min(512, t)
    tl = min(256, seq)
    rows = min(SSD_CHUNK, seq)
    vl = None if valid_len == seq else valid_len
    kv_in_kernel = valid_len == seq and seq % min(256, t) == 0
    states = []
    for li in range(DEPTH):
        ln = (W['ln_in_g'], W['ln_in_b']) if li == 0 else None
        tp = min(256, t)
        if kv_in_kernel:
            res = in_proj(x, P['w_in'][li], tp, li, ln=ln, kv=None if li == 0 else (kt_all, vt_all), seq=seq)
            kt_all, vt_all = res[-2:]
            res = res[:-2]
        else:
            res = in_proj(x, P['w_in'][li], tp, li, ln=ln)
        if li == 0:
            x = res[0]
        proj, qkv16 = res[-2:]
        y_lru, lru_ht = lru_mixer(proj, nb, seq, tl, _pad_prev(lru_buf0[li]), lru_h0[li][:, None, :],
                                  P['lru_cw'][li], W['lru_conv_b'][li][None], P['lru_wgate'][li],
                                  P['lru_bgate'][li], W['lru_lambda'][li][None], at_pos0, vl)
        y_ssd, ssd_st = ssd_mixer(proj, nb, seq, rows, _pad_prev(ssd_buf0[li]), ssd_s0[li],
                                  P['ssd_cw'][li], W['ssd_conv_b'][li][None], P['ssd_dtb'][li], P['ssd_a'][li],
                                  P['ssd_d'][li], W['ssd_norm_w'][li][None], vl)
        y_sb = sb_fn(li, proj, qkv16)
        x1 = outproj_ln1(y_lru, y_sb, y_ssd, x, P['w_out'][li], W['ln1_g'][li], W['ln1_b'][li], tm)
        epilogue = (p[li], P['ple_w_gate'][li], P['ple_w_proj'][li], W['ln2_g'][li][None], W['ln2_b'][li][None])
        j = li // 2
        if li % 2 == 0:
            x = ffn_dense(x1, P['ffn_w_gate'][j], P['ffn_w_up'][j], P['ffn_w_down'][j], epilogue, tm, 1408)
        else:
            gates, sel, tile_counts = router_gates(x1, W['moe_router'][j], tm)
            if t >= MOE_SPARSE_MIN_TOKENS:
                x = moe_sparse(x1, sel, tile_counts, P['moe_w_gate'][j], P['moe_w_up'][j], P['moe_w_down'][j],
                               tm, 1792, epilogue)
            else:
                f = moe_dense(x1, gates, P['moe_w_gate'][j], P['moe_w_up'][j], P['moe_w_down'][j], tm, 1792)
                x = ple_ln2(x1, f, epilogue, tm)
        p3 = proj.reshape(nb, seq, C_END)
        tail = slice(valid_len - (CONV_W - 1), valid_len)
        heads = lambda c: p3[:, :valid_len, c:c + SB_WIDTH].reshape(nb, valid_len, SB_HEADS, SB_HEAD_DIM)
        states.append((None if kv_in_kernel else heads(C_K), None if kv_in_kernel else heads(C_V),
                       lru_ht[:, 0], p3[:, tail, C_XL:C_XL + LRU_WIDTH], ssd_st,
                       p3[:, tail, C_XBC:C_XBC + SSD_CONV_DIM]))
    stacked = tuple(jnp.stack([st[n] for st in states]) for n in range(2 if kv_in_kernel else 0, 6))
    if kv_in_kernel:
        seq_last = lambda a: jnp.transpose(a.reshape(DEPTH, nb, SB_HEADS, SB_HEAD_DIM, seq), (0, 1, 4, 2, 3))
        stacked = (seq_last(kt_all), seq_last(vt_all)) + stacked
    return x, stacked


def kernel(x_prompt, x_sample, p_prompt, p_sample, cache_k, cache_v, page_table, state_lru_h, state_lru_conv, state_ssd, state_ssd_conv, ln_in_g, ln_in_b, w_in, lru_conv_w, lru_conv_b, lru_wa, lru_ba, lru_wx, lru_bx, lru_lambda, sb_bias, ssd_conv_w, ssd_conv_b, ssd_dt_bias, ssd_a_log, ssd_d, ssd_norm_w, w_out, ln1_g, ln1_b, ln2_g, ln2_b, ffn_w_gate, ffn_w_up, ffn_w_down, moe_router, moe_w_gate, moe_w_up, moe_w_down, ple_w_gate, ple_w_proj):
    W = dict(ln_in_g=ln_in_g, ln_in_b=ln_in_b, w_in=w_in, lru_conv_w=lru_conv_w, lru_conv_b=lru_conv_b,
             lru_wa=lru_wa, lru_ba=lru_ba, lru_wx=lru_wx, lru_bx=lru_bx, lru_lambda=lru_lambda,
             sb_bias=sb_bias, ssd_conv_w=ssd_conv_w, ssd_conv_b=ssd_conv_b, ssd_dt_bias=ssd_dt_bias,
             ssd_a_log=ssd_a_log, ssd_d=ssd_d, ssd_norm_w=ssd_norm_w, w_out=w_out, ln1_g=ln1_g,
             ln1_b=ln1_b, ln2_g=ln2_g, ln2_b=ln2_b, ffn_w_gate=ffn_w_gate, ffn_w_up=ffn_w_up,
             ffn_w_down=ffn_w_down, moe_router=moe_router, moe_w_gate=moe_w_gate, moe_w_up=moe_w_up,
             moe_w_down=moe_w_down, ple_w_gate=ple_w_gate, ple_w_proj=ple_w_proj)
    P = _prep_weights(W)
    d = D_MODEL

    bp, sp, _ = x_prompt.shape
    zeros = lambda *s: jnp.zeros((DEPTH, bp) + s, F32)

    def prompt_sb(li, proj, qkv16):
        del proj
        return sb_prompt(qkv16, sb_bias[li], bp, sp)

    y_prompt, st_p = _run_group(
        x_prompt.reshape(bp * sp, d), p_prompt.reshape(DEPTH, bp * sp, PLE_DIM), bp, sp, sp, True,
        zeros(LRU_WIDTH), zeros(CONV_W - 1, LRU_WIDTH), zeros(SSD_HEADS, SSD_HEAD_DIM, SSD_D_STATE),
        zeros(CONV_W - 1, SSD_CONV_DIM), prompt_sb, W, P)

    bs, ts, _ = x_sample.shape
    rs = SAMPLE_ROWS
    pad_rows = lambda a, ax: jnp.pad(a, [(0, rs - ts) if n == ax else (0, 0) for n in range(a.ndim)])
    scale = 1.0 / math.sqrt(SB_HEAD_DIM)
    cache_k_t = jnp.transpose(cache_k, (0, 1, 3, 4, 2))
    cache_v_t = jnp.transpose(cache_v, (0, 1, 3, 4, 2))
    row_valid = (jnp.arange(rs) < ts)[None, None, :, None]

    def sample_sb(li, proj, qkv16):
        del qkv16
        p3 = proj.reshape(bs, rs, C_END)
        heads = lambda c: p3[:, :, c:c + SB_WIDTH].reshape(bs, rs, SB_HEADS, SB_HEAD_DIM)
        q3 = jnp.where(row_valid, jnp.transpose(heads(C_Q), (0, 2, 1, 3)) * scale, 0.0).astype(BF16)
        new_t = lambda c: jnp.pad(jnp.transpose(heads(c), (0, 2, 3, 1)), ((0, 0),) * 3 + ((0, PAGE_SIZE - rs),))
        out = sb_sample(q3, sb_bias[li][:, None, None], new_t(C_K), new_t(C_V), cache_k_t, cache_v_t,
                        page_table, li)
        return jnp.transpose(out, (0, 2, 1, 3)).reshape(bs * rs, SB_WIDTH)

    y_sample, st_s = _run_group(
        pad_rows(x_sample, 1).reshape(bs * rs, d), pad_rows(p_sample, 2).reshape(DEPTH, bs * rs, PLE_DIM),
        bs, rs, ts, False, state_lru_h, state_lru_conv, state_ssd, state_ssd_conv, sample_sb, W, P)
    y_sample = y_sample.reshape(bs, rs, d)[:, :ts]

    return (y_prompt.reshape(bp, sp, d), y_sample) + st_p + st_s
```

```python
import functools
import math

import jax
import jax.numpy as jnp
from jax import lax
from jax.experimental import pallas as pl
from jax.experimental.pallas import tpu as pltpu

F32 = jnp.float32
BF16 = jnp.bfloat16

D_MODEL = 1024
DEPTH = 2
PAGE_SIZE = 128
LRU_WIDTH = 256
LRU_BLOCKS = 4
LRU_C = 8.0
SB_HEADS = 8
SB_HEAD_DIM = 64
SB_WIDTH = SB_HEADS * SB_HEAD_DIM
SSD_HEADS = 4
SSD_HEAD_DIM = 64
SSD_D_INNER = SSD_HEADS * SSD_HEAD_DIM
SSD_GROUPS = 2
SSD_D_STATE = 64
SSD_CONV_DIM = SSD_D_INNER + 2 * SSD_GROUPS * SSD_D_STATE
SSD_CHUNK = 128
CONV_W = 4
N_EXPERTS = 8
PLE_DIM = 256
DN_ALPHA = (2 * DEPTH) ** 0.25
LN_EPS = 1e-5
RMS_EPS = 1e-6
LOG2E = math.log2(math.e)

LANES = 128
SUBLANES = 8
VMEM_LIMIT = 56 * 1024 * 1024

C_XL, C_GL, C_Q, C_K, C_V, C_XBC, C_Z, C_DT, C_END = 0, 256, 512, 1024, 1536, 2048, 2560, 2816, 3072
SAMPLE_ROWS = 8
SB_TILE = 256
PAGES_PER_STEP = 32
MOE_SPARSE_MIN_TOKENS = 4096


def _cp(*sem):
    return pltpu.CompilerParams(dimension_semantics=sem, vmem_limit_bytes=VMEM_LIMIT)


def _layer_norm(x, g, b):
    mu = jnp.mean(x, axis=-1, keepdims=True)
    xc = x - mu
    var = jnp.mean(xc * xc, axis=-1, keepdims=True)
    return xc * lax.rsqrt(var + LN_EPS) * g + b


def _softplus(x):
    return jnp.maximum(x, 0.0) + jnp.log(1.0 + jnp.exp(-jnp.abs(x)))


def _sigmoid(x):
    return 1.0 / (1.0 + jnp.exp(-x))


def _silu(x):
    return x * _sigmoid(x)


def _gelu_tanh(x):
    return 0.5 * x * (1.0 + jnp.tanh(math.sqrt(2.0 / math.pi) * (x + 0.044715 * (x * x * x))))


def _row_iota(shape):
    return lax.broadcasted_iota(jnp.int32, shape, 0)


def _col_iota(shape):
    return lax.broadcasted_iota(jnp.int32, shape, 1)


def _shift_rows_prev(u, prev8, k):
    r = pltpu.roll(u, k, 0)
    head = jnp.where(_row_iota(prev8.shape) < k, pltpu.roll(prev8, k, 0), r[:SUBLANES])
    if u.shape[0] == SUBLANES:
        return head
    return jnp.concatenate([head, r[SUBLANES:]], axis=0)


def _causal_conv(u, prev8, cw, cb):
    out = cb + cw[3:4] * u
    for k in range(1, CONV_W):
        out = out + cw[CONV_W - 1 - k:CONV_W - k] * _shift_rows_prev(u, prev8, k)
    return out


def _shift_rows_fill(x, d, fill):
    return jnp.where(_row_iota(x.shape) >= d, pltpu.roll(x, d, 0), fill)


def _inproj_tail(x, w_ref, o_ref, qkv_ref):
    y = jnp.dot(x.astype(BF16), w_ref[...], preferred_element_type=F32)
    o_ref[...] = y
    qkv_ref[:, :SB_WIDTH] = (y[:, C_Q:C_Q + SB_WIDTH] * (LOG2E / math.sqrt(SB_HEAD_DIM))).astype(BF16)
    qkv_ref[:, SB_WIDTH:] = y[:, C_K:C_V + SB_WIDTH].astype(BF16)
    return y[:, C_K:C_K + SB_WIDTH], y[:, C_V:C_V + SB_WIDTH]


def _inproj_body(x_ref, w_ref, o_ref, qkv_ref):
    _inproj_tail(x_ref[...], w_ref, o_ref, qkv_ref)


def _ln_inproj_body(x_ref, g_ref, b_ref, w_ref, xn_ref, o_ref, qkv_ref):
    xn = _layer_norm(x_ref[...], g_ref[...], b_ref[...])
    xn_ref[...] = xn
    _inproj_tail(xn, w_ref, o_ref, qkv_ref)


def _inproj_kv_body(x_ref, w_ref, k_in, v_in, o_ref, qkv_ref, kt_ref, vt_ref):
    del k_in, v_in
    k, v = _inproj_tail(x_ref[...], w_ref, o_ref, qkv_ref)
    kt_ref[...], vt_ref[...] = k.T, v.T


def _ln_inproj_kv_body(x_ref, g_ref, b_ref, w_ref, xn_ref, o_ref, qkv_ref, kt_ref, vt_ref):
    xn = _layer_norm(x_ref[...], g_ref[...], b_ref[...])
    xn_ref[...] = xn
    k, v = _inproj_tail(xn, w_ref, o_ref, qkv_ref)
    kt_ref[0], vt_ref[0] = k.T, v.T
    kt_ref[1:] = jnp.zeros_like(kt_ref[1:])
    vt_ref[1:] = jnp.zeros_like(vt_ref[1:])


def in_proj(x, w, tm, layer, ln=None, kv=None, seq=None):
    t, k = x.shape
    n = w.shape[1]
    rows = lambda c: pl.BlockSpec((tm, c), lambda i: (i, 0))
    const = lambda r, c: pl.BlockSpec((r, c), lambda i: (0, 0))
    ins, args = [rows(k), const(k, n)], (x, w)
    outs = [(rows(n), jax.ShapeDtypeStruct((t, n), F32)), (rows(3 * SB_WIDTH), jax.ShapeDtypeStruct((t, 3 * SB_WIDTH), BF16))]
    aliases = {}
    if ln is not None:
        assert layer == 0 and kv is None
        ins, args = [rows(k), const(1, k), const(1, k), const(k, n)], (x, ln[0][None], ln[1][None], w)
        outs = [(rows(k), jax.ShapeDtypeStruct((t, k), F32))] + outs
    if seq is None:
        body = _inproj_body if ln is None else _ln_inproj_body
    else:
        per_seq = seq // tm
        stack = jax.ShapeDtypeStruct((DEPTH, t // seq, SB_WIDTH, seq), F32)
        if ln is None:
            body = _inproj_kv_body
            spec = pl.BlockSpec((None, None, SB_WIDTH, tm), lambda i: (layer, i // per_seq, 0, i % per_seq))
            any_spec = pl.BlockSpec(memory_space=pl.ANY)
            ins, args = ins + [any_spec, any_spec], args + tuple(kv)
            aliases = {len(ins) - 2: len(outs), len(ins) - 1: len(outs) + 1}
        else:
            body = _ln_inproj_kv_body
            spec = pl.BlockSpec((DEPTH, None, SB_WIDTH, tm), lambda i: (0, i // per_seq, 0, i % per_seq))
        outs = outs + [(spec, stack), (spec, stack)]
    return pl.pallas_call(
        body, grid=(t // tm,), in_specs=ins, out_specs=[o[0] for o in outs], out_shape=[o[1] for o in outs],
        input_output_aliases=aliases, compiler_params=_cp("parallel"), name="in_proj")(*args)


def _outproj_body(yl_ref, ys_ref, yd_ref, x_ref, w_ref, g_ref, b_ref, o_ref):
    mixed = jnp.concatenate([yl_ref[...], ys_ref[...], yd_ref[...]], axis=-1).astype(BF16)
    mix = jnp.dot(mixed, w_ref[...], preferred_element_type=F32)
    o_ref[...] = _layer_norm(DN_ALPHA * x_ref[...] + mix, g_ref[...], b_ref[...])


def outproj_ln1(y_lru, y_sb, y_ssd, x, w, g, b, tm):
    t, d = x.shape
    rows = lambda c: pl.BlockSpec((tm, c), lambda i: (i, 0))
    vec = pl.BlockSpec((1, d), lambda i: (0, 0))
    return pl.pallas_call(
        _outproj_body, grid=(t // tm,),
        in_specs=[rows(LRU_WIDTH), rows(SB_WIDTH), rows(SSD_D_INNER), rows(d),
                  pl.BlockSpec((d, d), lambda i: (0, 0)), vec, vec],
        out_specs=rows(d), out_shape=jax.ShapeDtypeStruct((t, d), F32),
        compiler_params=_cp("parallel"), name="outproj_ln1")(y_lru, y_sb, y_ssd, x, w, g.reshape(1, d), b.reshape(1, d))


def _ffn_body(x_ref, wg_ref, wu_ref, wd_ref, p_ref, pg_ref, pp_ref, g_ref, b_ref, o_ref, acc_s):
    f = pl.program_id(1)

    @pl.when(f == 0)
    def _():
        acc_s[...] = jnp.zeros_like(acc_s)

    xb = x_ref[...].astype(BF16)
    g = jnp.dot(xb, wg_ref[...], preferred_element_type=F32)
    u = jnp.dot(xb, wu_ref[...], preferred_element_type=F32)
    h = (_silu(g) * u).astype(BF16)
    acc_s[...] += jnp.dot(h, wd_ref[...], preferred_element_type=F32)

    @pl.when(f == pl.num_programs(1) - 1)
    def _():
        o_ref[...] = _ple_ln2(x_ref[...], acc_s[...], p_ref[...], pg_ref[...], pp_ref[...], g_ref[...], b_ref[...])


def ffn_dense(x, wg, wu, wd, epilogue, tm, tf):
    t, d = x.shape
    ff = wg.shape[1]
    return pl.pallas_call(
        _ffn_body, grid=(t // tm, ff // tf),
        in_specs=[pl.BlockSpec((tm, d), lambda i, f: (i, 0)),
                  pl.BlockSpec((d, tf), lambda i, f: (0, f)),
                  pl.BlockSpec((d, tf), lambda i, f: (0, f)),
                  pl.BlockSpec((tf, d), lambda i, f: (f, 0))] + _epilogue_specs(tm, d, lambda i, f: (i, 0)),
        out_specs=pl.BlockSpec((tm, d), lambda i, f: (i, 0)),
        out_shape=jax.ShapeDtypeStruct((t, d), F32),
        scratch_shapes=[pltpu.VMEM((tm, d), F32)],
        compiler_params=_cp("parallel", "arbitrary"), name="ffn_dense")(x, wg, wu, wd, *epilogue)


def _router_body(x_ref, w_ref, o_ref, sel_ref, cnt_ref):
    x = x_ref[...]
    w = w_ref[...]
    xh = x.astype(BF16)
    xl = (x - xh.astype(F32)).astype(BF16)
    wh = w.astype(BF16)
    wl = (w - wh.astype(F32)).astype(BF16)
    logits = (jnp.dot(xh, wh, preferred_element_type=F32) + jnp.dot(xh, wl, preferred_element_type=F32)
              + jnp.dot(xl, wh, preferred_element_type=F32))
    lane = _col_iota(logits.shape).astype(F32)
    logits = jnp.where(lane < N_EXPERTS, logits, -jnp.inf)
    m1 = jnp.max(logits, axis=-1, keepdims=True)
    i1 = jnp.min(jnp.where(logits == m1, lane, float(LANES)), axis=-1, keepdims=True)
    rest = jnp.where(lane == i1, -jnp.inf, logits)
    m2 = jnp.max(rest, axis=-1, keepdims=True)
    i2 = jnp.min(jnp.where(rest == m2, lane, float(LANES)), axis=-1, keepdims=True)
    e2 = jnp.exp(m2 - m1)
    w1 = 1.0 / (1.0 + e2)
    w2 = e2 * w1
    o_ref[...] = jnp.where(lane == i1, w1, 0.0) + jnp.where(lane == i2, w2, 0.0)
    tm = logits.shape[0]
    hit1 = jnp.where(lane == i1, 1.0, 0.0)
    hit2 = jnp.where(lane == i2, 1.0, 0.0)
    member = hit1 + hit2
    before = jnp.where(_row_iota((tm, tm)) > _col_iota((tm, tm)), 1.0, 0.0).astype(BF16)
    earlier = jnp.dot(before, member.astype(BF16), preferred_element_type=F32)
    r1 = jnp.sum(hit1 * earlier, axis=-1, keepdims=True)
    r2 = jnp.sum(hit2 * earlier, axis=-1, keepdims=True)
    cnt_ref[...] = jnp.sum(member, axis=0, keepdims=True)
    sel_ref[...] = (jnp.where(lane == 0, i1, 0.0) + jnp.where(lane == 1, i2, 0.0)
                    + jnp.where(lane == 2, w1, 0.0) + jnp.where(lane == 3, w2, 0.0)
                    + jnp.where(lane == 4, r1, 0.0) + jnp.where(lane == 5, r2, 0.0))


def router_gates(x, w_router, tm):
    t, d = x.shape
    wpad = jnp.pad(w_router, ((0, 0), (0, LANES - N_EXPERTS)))
    out = pl.BlockSpec((tm, LANES), lambda i: (i, 0))
    return pl.pallas_call(
        _router_body, grid=(t // tm,),
        in_specs=[pl.BlockSpec((tm, d), lambda i: (i, 0)), pl.BlockSpec((d, LANES), lambda i: (0, 0))],
        out_specs=[out, out, pl.BlockSpec((None, 1, LANES), lambda i: (i, 0, 0))],
        out_shape=[jax.ShapeDtypeStruct((t, LANES), F32)] * 2 + [jax.ShapeDtypeStruct((t // tm, 1, LANES), F32)],
        compiler_params=_cp("parallel"), name="router")(x, wpad)


def _moe_body(x_ref, gates_ref, wg_ref, wu_ref, wd_ref, o_ref, acc_s):
    e = pl.program_id(1)
    f = pl.program_id(2)

    @pl.when(jnp.logical_and(e == 0, f == 0))
    def _():
        acc_s[...] = jnp.zeros_like(acc_s)

    xb = x_ref[...].astype(BF16)
    g = jnp.dot(xb, wg_ref[...], preferred_element_type=F32)
    u = jnp.dot(xb, wu_ref[...], preferred_element_type=F32)
    gates = gates_ref[...]
    gate = jnp.sum(jnp.where(_col_iota(gates.shape) == e, gates, 0.0), axis=-1, keepdims=True)
    h = (_silu(g) * u * gate).astype(BF16)
    acc_s[...] += jnp.dot(h, wd_ref[...], preferred_element_type=F32)

    @pl.when(jnp.logical_and(e == pl.num_programs(1) - 1, f == pl.num_programs(2) - 1))
    def _():
        o_ref[...] = acc_s[...]


def moe_dense(x, gates, wg, wu, wd, tm, tf):
    t, d = x.shape
    ne, _, ff = wg.shape
    return pl.pallas_call(
        _moe_body, grid=(t // tm, ne, ff // tf),
        in_specs=[pl.BlockSpec((tm, d), lambda i, e, f: (i, 0)),
                  pl.BlockSpec((tm, LANES), lambda i, e, f: (i, 0)),
                  pl.BlockSpec((None, d, tf), lambda i, e, f: (e, 0, f)),
                  pl.BlockSpec((None, d, tf), lambda i, e, f: (e, 0, f)),
                  pl.BlockSpec((None, tf, d), lambda i, e, f: (e, f, 0))],
        out_specs=pl.BlockSpec((tm, d), lambda i, e, f: (i, 0)),
        out_shape=jax.ShapeDtypeStruct((t, d), F32),
        scratch_shapes=[pltpu.VMEM((tm, d), F32)],
        compiler_params=_cp("parallel", "arbitrary", "arbitrary"), name="moe_dense")(x, gates, wg, wu, wd)


def _dispatch_body(dest_ref, x_ref, zeros_ref, xs_ref, sem, *, tm):
    del zeros_ref

    def row_copy(r, s):
        return pltpu.make_async_copy(x_ref.at[pl.ds(r, 1)], xs_ref.at[pl.ds(dest_ref[2 * r + s], 1)], sem)

    def issue(r, carry):
        row_copy(r, 0).start()
        row_copy(r, 1).start()
        return carry

    def drain(r, carry):
        row_copy(r, 0).wait()
        row_copy(r, 1).wait()
        return carry

    lax.fori_loop(0, tm, issue, 0, unroll=8)
    lax.fori_loop(0, tm, drain, 0, unroll=8)


def moe_dispatch(x, dest, n_rows, tm):
    t, d = x.shape
    return pl.pallas_call(
        functools.partial(_dispatch_body, tm=tm), grid=(t // tm,),
        in_specs=[pl.BlockSpec((2 * tm,), lambda i: (i,), memory_space=pltpu.SMEM),
                  pl.BlockSpec((tm, d), lambda i: (i, 0)),
                  pl.BlockSpec(memory_space=pl.ANY)],
        out_specs=pl.BlockSpec(memory_space=pl.ANY),
        out_shape=jax.ShapeDtypeStruct((n_rows, d), F32),
        scratch_shapes=[pltpu.SemaphoreType.DMA(())],
        input_output_aliases={2: 0},
        compiler_params=_cp("arbitrary"), name="moe_dispatch")(dest, x, jnp.zeros((n_rows, d), F32))


def _moe_group_body(te_ref, nu_ref, x_ref, wg_ref, wu_ref, wd_ref, o_ref, acc_s):
    del te_ref
    i = pl.program_id(0)
    f = pl.program_id(1)
    last = f == pl.num_programs(1) - 1

    @pl.when(i < nu_ref[0])
    def _():
        xb = x_ref[...].astype(BF16)
        g = jnp.dot(xb, wg_ref[...], preferred_element_type=F32)
        u = jnp.dot(xb, wu_ref[...], preferred_element_type=F32)
        y = jnp.dot((_silu(g) * u).astype(BF16), wd_ref[...], preferred_element_type=F32)

        @pl.when(f == 0)
        def _():
            acc_s[...] = y

        @pl.when(f > 0)
        def _():
            acc_s[...] += y

        @pl.when(last)
        def _():
            o_ref[...] = acc_s[...]

    @pl.when(jnp.logical_and(i >= nu_ref[0], last))
    def _():
        o_ref[...] = jnp.zeros_like(o_ref)


def moe_grouped(xs, tile_expert, n_used, wg, wu, wd, tm, tf):
    r, d = xs.shape
    ff = wg.shape[2]
    grid_spec = pltpu.PrefetchScalarGridSpec(
        num_scalar_prefetch=2, grid=(r // tm, ff // tf),
        in_specs=[pl.BlockSpec((tm, d), lambda i, f, te, nu: (i, 0)),
                  pl.BlockSpec((None, d, tf), lambda i, f, te, nu: (te[i], 0, f)),
                  pl.BlockSpec((None, d, tf), lambda i, f, te, nu: (te[i], 0, f)),
                  pl.BlockSpec((None, tf, d), lambda i, f, te, nu: (te[i], f, 0))],
        out_specs=pl.BlockSpec((tm, d), lambda i, f, te, nu: (i, 0)),
        scratch_shapes=[pltpu.VMEM((tm, d), F32)])
    return pl.pallas_call(
        _moe_group_body, grid_spec=grid_spec, out_shape=jax.ShapeDtypeStruct((r, d), F32),
        compiler_params=_cp("arbitrary", "arbitrary"), name="moe_grouped")(tile_expert, n_used, xs, wg, wu, wd)


def _combine_body(dest_ref, sel_ref, ys_ref, x_ref, p_ref, pg_ref, pp_ref, g_ref, b_ref, o_ref, buf0, buf1, sem,
                  *, tm):
    def row_copy(r, s):
        buf = buf0 if s == 0 else buf1
        return pltpu.make_async_copy(ys_ref.at[pl.ds(dest_ref[2 * r + s], 1)], buf.at[pl.ds(r, 1)], sem)

    def issue(r, carry):
        row_copy(r, 0).start()
        row_copy(r, 1).start()
        return carry

    def drain(r, carry):
        row_copy(r, 0).wait()
        row_copy(r, 1).wait()
        return carry

    lax.fori_loop(0, tm, issue, 0, unroll=8)
    lax.fori_loop(0, tm, drain, 0, unroll=8)
    sel = sel_ref[...]
    f = sel[:, 2:3] * buf0[...] + sel[:, 3:4] * buf1[...]
    o_ref[...] = _ple_ln2(x_ref[...], f, p_ref[...], pg_ref[...], pp_ref[...], g_ref[...], b_ref[...])


def moe_combine(ys, dest, sel, x1, tm, epilogue):
    t, d = x1.shape
    return pl.pallas_call(
        functools.partial(_combine_body, tm=tm), grid=(t // tm,),
        in_specs=[pl.BlockSpec((2 * tm,), lambda i: (i,), memory_space=pltpu.SMEM),
                  pl.BlockSpec((tm, LANES), lambda i: (i, 0)),
                  pl.BlockSpec(memory_space=pl.ANY),
                  pl.BlockSpec((tm, d), lambda i: (i, 0))] + _epilogue_specs(tm, d, lambda i: (i, 0)),
        out_specs=pl.BlockSpec((tm, d), lambda i: (i, 0)),
        out_shape=jax.ShapeDtypeStruct((t, d), F32),
        scratch_shapes=[pltpu.VMEM((tm, d), F32), pltpu.VMEM((tm, d), F32), pltpu.SemaphoreType.DMA(())],
        compiler_params=_cp("arbitrary"), name="moe_combine")(dest, sel, ys, x1, *epilogue)


def moe_sparse(x, sel, tile_counts, wg, wu, wd, tm, tf, epilogue):
    t, _ = x.shape
    ne = wg.shape[0]
    cnt = tile_counts[:, 0, :ne].astype(jnp.int32)
    before_tile = jnp.cumsum(cnt, axis=0) - cnt
    counts = jnp.sum(cnt, axis=0)
    padded = ((counts + tm - 1) // tm) * tm
    ends = jnp.cumsum(padded)
    base = jnp.repeat((ends - padded)[None, :] + before_tile, tm, axis=0)
    experts = jnp.arange(ne, dtype=jnp.int32)[None, :]

    def dest_of(slot):
        hit = sel[:, slot:slot + 1].astype(jnp.int32) == experts
        return jnp.sum(jnp.where(hit, base, 0), axis=1) + sel[:, 4 + slot].astype(jnp.int32)

    dest = jnp.stack([dest_of(0), dest_of(1)], axis=1).reshape(-1)
    n_tiles = (2 * t) // tm + ne
    tile_start = jnp.arange(n_tiles, dtype=jnp.int32) * tm
    tile_expert = jnp.minimum(jnp.sum((tile_start[:, None] >= ends[None, :]).astype(jnp.int32), axis=1), ne - 1)
    n_used = (ends[-1:] // tm).astype(jnp.int32)
    xs = moe_dispatch(x, dest, n_tiles * tm, tm)
    ys = moe_grouped(xs, tile_expert.astype(jnp.int32), n_used, wg, wu, wd, tm, tf)
    return moe_combine(ys, dest, sel, x, tm, epilogue)


def _ple_ln2(x1, f, p, wg, wp, g, b):
    gate = _sigmoid(jnp.dot(x1.astype(BF16), wg, preferred_element_type=F32))
    proj = jnp.dot(p.astype(BF16), wp, preferred_element_type=F32)
    return _layer_norm(DN_ALPHA * x1 + f + gate * proj, g, b)


def _ple_ln2_body(x_ref, f_ref, p_ref, wg_ref, wp_ref, g_ref, b_ref, o_ref):
    o_ref[...] = _ple_ln2(x_ref[...], f_ref[...], p_ref[...], wg_ref[...], wp_ref[...], g_ref[...], b_ref[...])


def _epilogue_specs(tm, d, idx):
    const = lambda *_: (0, 0)
    return [pl.BlockSpec((tm, PLE_DIM), idx), pl.BlockSpec((d, d), const), pl.BlockSpec((PLE_DIM, d), const),
            pl.BlockSpec((1, d), const), pl.BlockSpec((1, d), const)]


def ple_ln2(x1, f, epilogue, tm):
    t, d = x1.shape
    rows = lambda c: pl.BlockSpec((tm, c), lambda i: (i, 0))
    return pl.pallas_call(
        _ple_ln2_body, grid=(t // tm,),
        in_specs=[rows(d), rows(d)] + _epilogue_specs(tm, d, lambda i: (i, 0)),
        out_specs=rows(d), out_shape=jax.ShapeDtypeStruct((t, d), F32),
        compiler_params=_cp("parallel"), name="ple_ln2")(x1, f, *epilogue)


def _lru_body(x_ref, g_ref, prev0_ref, h0_ref, cw_ref, cb_ref, wg_ref, bg_ref, lam_ref,
              y_ref, ht_ref, prev_s, h_s, *, tl, first_is_pos0, valid_len):
    i = pl.program_id(1)

    @pl.when(i == 0)
    def _():
        prev_s[...] = prev0_ref[...]
        h_s[...] = h0_ref[...]

    u = x_ref[...]
    xc = _causal_conv(u, prev_s[...], cw_ref[...], cb_ref[...])
    prev_s[...] = u[tl - SUBLANES:tl]

    gates = jnp.dot(xc.astype(BF16), wg_ref[...], preferred_element_type=F32) + bg_ref[...]
    r = _sigmoid(gates[:, :LRU_WIDTH])
    ig = _sigmoid(gates[:, LRU_WIDTH:])
    a = jnp.exp(-LRU_C * r * _softplus(-lam_ref[...]))
    mult = jnp.sqrt(1.0 - a * a)
    row = i * tl + _row_iota(u.shape)
    if first_is_pos0:
        mult = jnp.where(row == 0, 1.0, mult)
    b = mult * ig * xc
    if valid_len is not None:
        valid = row < valid_len
        a = jnp.where(valid, a, 1.0)
        b = jnp.where(valid, b, 0.0)

    d = 1
    while d < tl:
        b = b + a * _shift_rows_fill(b, d, 0.0)
        a = a * _shift_rows_fill(a, d, 1.0)
        d *= 2
    hs = b + a * h_s[...]
    h_last = hs[tl - 1:tl]
    h_s[...] = h_last
    ht_ref[...] = h_last
    y_ref[...] = hs * _gelu_tanh(g_ref[...])


def lru_mixer(proj, nb, seq, tl, prev0, h0, cw, cb, wgate, bgate, lam, first_is_pos0, valid_len):
    nl = seq // tl
    w = LRU_WIDTH
    vec = lambda r, c: pl.BlockSpec((r, c), lambda b, i: (0, 0))
    body = functools.partial(_lru_body, tl=tl, first_is_pos0=first_is_pos0, valid_len=valid_len)
    return pl.pallas_call(
        body, grid=(nb, nl),
        in_specs=[pl.BlockSpec((tl, w), lambda b, i: (b * nl + i, C_XL // w)),
                  pl.BlockSpec((tl, w), lambda b, i: (b * nl + i, C_GL // w)),
                  pl.BlockSpec((None, SUBLANES, w), lambda b, i: (b, 0, 0)),
                  pl.BlockSpec((None, 1, w), lambda b, i: (b, 0, 0)),
                  vec(SUBLANES, w), vec(1, w), vec(w, 2 * w), vec(1, 2 * w), vec(1, w)],
        out_specs=[pl.BlockSpec((tl, w), lambda b, i: (b * nl + i, 0)),
                   pl.BlockSpec((None, 1, w), lambda b, i: (b, 0, 0))],
        out_shape=[jax.ShapeDtypeStruct((nb * seq, w), F32), jax.ShapeDtypeStruct((nb, 1, w), F32)],
        scratch_shapes=[pltpu.VMEM((SUBLANES, w), F32), pltpu.VMEM((1, w), F32)],
        compiler_params=_cp("parallel", "arbitrary"), name="lru")(
            proj, proj, prev0, h0, cw, cb, wgate, bgate, lam)


def _ssd_body(xbc_ref, z_ref, dt_ref, prev0_ref, s0_ref, cw_ref, cb_ref, dtb_ref, a_ref, d_ref, nw_ref,
              y_ref, st_ref, prev_s, s_s, *, rows, valid_len):
    i = pl.program_id(1)
    q = SSD_CHUNK
    hp = SSD_HEAD_DIM
    ns = SSD_D_STATE

    @pl.when(i == 0)
    def _():
        prev_s[...] = prev0_ref[...]
        s_s[...] = s0_ref[...]

    u = xbc_ref[...]
    xbc = _silu(_causal_conv(u, prev_s[...], cw_ref[...], cb_ref[...]))
    prev_s[...] = u[rows - SUBLANES:rows]
    dt = _softplus(dt_ref[...] + dtb_ref[...])
    if valid_len is not None:
        dt = jnp.where(i * rows + _row_iota(dt.shape) < valid_len, dt, 0.0)
    if rows < q:
        xbc = jnp.concatenate([xbc, jnp.zeros((q - rows, xbc.shape[1]), F32)], axis=0)
        dt = jnp.concatenate([dt, jnp.zeros((q - rows, dt.shape[1]), F32)], axis=0)
    xs = xbc[:, :SSD_D_INNER]
    bm = xbc[:, SSD_D_INNER:SSD_D_INNER + SSD_GROUPS * ns]
    cm = xbc[:, SSD_D_INNER + SSD_GROUPS * ns:]

    a_cs = dt * a_ref[...]
    d = 1
    while d < q:
        a_cs = a_cs + _shift_rows_fill(a_cs, d, 0.0)
        d *= 2
    a_cs_t = a_cs.T
    tri = _row_iota((q, q)) >= _col_iota((q, q))

    rep = SSD_HEADS // SSD_GROUPS
    nt = (((1,), (1,)), ((), ()))
    tn = (((0,), (0,)), ((), ()))
    cb_g = [lax.dot_general(cm[:, g * ns:(g + 1) * ns], bm[:, g * ns:(g + 1) * ns], nt,
                            preferred_element_type=F32) for g in range(SSD_GROUPS)]
    ys = []
    for h in range(SSD_HEADS):
        g = h // rep
        col = a_cs[:, h:h + 1]
        seg = col - a_cs_t[h:h + 1, :]
        lmat = jnp.exp(jnp.where(tri, seg, -jnp.inf))
        xs_h = xs[:, h * hp:(h + 1) * hp]
        xd = xs_h * dt[:, h:h + 1]
        c_g = cm[:, g * ns:(g + 1) * ns]
        b_g = bm[:, g * ns:(g + 1) * ns]
        s_prev = s_s[h]
        y_diag = jnp.dot(cb_g[g] * lmat, xd, preferred_element_type=F32)
        y_off = jnp.exp(col) * lax.dot_general(c_g, s_prev, nt, preferred_element_type=F32)
        a_last = a_cs[q - 1:q, h:h + 1]
        s_new = jnp.exp(a_last) * s_prev + lax.dot_general(xd * jnp.exp(a_last - col), b_g, tn,
                                                            preferred_element_type=F32)
        s_s[h] = s_new
        st_ref[h] = s_new
        ys.append(y_diag + y_off)
    y = jnp.concatenate(ys, axis=-1) + d_ref[...] * xs
    if rows < q:
        y = y[:rows]
    y = y * _silu(z_ref[...])
    y_ref[...] = y * lax.rsqrt(jnp.mean(y * y, axis=-1, keepdims=True) + RMS_EPS) * nw_ref[...]


def ssd_mixer(proj, nb, seq, rows, prev0, s0, cw, cb, dtb, a_neg, d_skip, norm_w, valid_len):
    nl = seq // rows
    cdim = SSD_CONV_DIM
    di = SSD_D_INNER
    vec = lambda r, c: pl.BlockSpec((r, c), lambda b, i: (0, 0))
    st = pl.BlockSpec((None, SSD_HEADS, SSD_HEAD_DIM, SSD_D_STATE), lambda b, i: (b, 0, 0, 0))
    body = functools.partial(_ssd_body, rows=rows, valid_len=valid_len)
    return pl.pallas_call(
        body, grid=(nb, nl),
        in_specs=[pl.BlockSpec((rows, cdim), lambda b, i: (b * nl + i, C_XBC // cdim)),
                  pl.BlockSpec((rows, di), lambda b, i: (b * nl + i, C_Z // di)),
                  pl.BlockSpec((rows, LANES), lambda b, i: (b * nl + i, C_DT // LANES)),
                  pl.BlockSpec((None, SUBLANES, cdim), lambda b, i: (b, 0, 0)),
                  st, vec(SUBLANES, cdim), vec(1, cdim), vec(1, LANES), vec(1, LANES), vec(1, di), vec(1, di)],
        out_specs=[pl.BlockSpec((rows, di), lambda b, i: (b * nl + i, 0)), st],
        out_shape=[jax.ShapeDtypeStruct((nb * seq, di), F32),
                   jax.ShapeDtypeStruct((nb, SSD_HEADS, SSD_HEAD_DIM, SSD_D_STATE), F32)],
        scratch_shapes=[pltpu.VMEM((SUBLANES, cdim), F32),
                        pltpu.VMEM((SSD_HEADS, SSD_HEAD_DIM, SSD_D_STATE), F32)],
        compiler_params=_cp("parallel", "arbitrary"), name="ssd")(
            proj, proj, proj, prev0, s0, cw, cb, dtb, a_neg, d_skip, norm_w)


def _neg_strict_upper(n):
    return jnp.where(_row_iota((n, n)) > _col_iota((n, n)), -1.0, 0.0).astype(BF16)


def _neg_abs(x):
    return pltpu.bitcast(pltpu.bitcast(x, jnp.uint32) | jnp.uint32(0x80000000), F32)


def _sb_prompt_body(bias_ref, q_ref, k_ref, v_ref, o_ref, c_s, rs_s, p_s, qm_s, *, tq):
    i = pl.program_id(1)
    hd = SB_HEAD_DIM
    neg_upper = _neg_strict_upper(tq)
    diag_mask = _col_iota((tq, tq)) < _row_iota((tq, tq))
    nt = (((1,), (1,)), ((), ()))
    H = SB_HEADS
    pw = 2 * hd
    low_half = _col_iota((tq, pw)) < hd
    lane_x = _col_iota((tq, pw))
    for h in range(H):
        qp = q_ref[:, (h // 2) * pw:(h // 2 + 1) * pw]
        qm = jnp.where(low_half if h % 2 == 0 else jnp.logical_not(low_half), qp, jnp.zeros_like(qp))
        b0 = jnp.full((tq, pw), bias_ref[h] * LOG2E, F32)
        b_hi = b0.astype(BF16).astype(F32)
        b_mid = (b0 - b_hi).astype(BF16).astype(F32)
        b_lo = b0 - b_hi - b_mid
        ext = jnp.where(lane_x == 0, b_hi, jnp.where(lane_x == 1, b_mid, jnp.where(lane_x == 2, b_lo, 0.0)))
        qm_s[h] = jnp.concatenate([qm, ext.astype(BF16)], axis=1)
    ones_ext = jnp.ones((tq, pw), BF16)

    def weights(kb, mask):
        r0 = pl.multiple_of(kb * tq, tq)
        zs = []
        for h in range(H):
            pcols = slice((h // 2) * pw, (h // 2 + 1) * pw)
            k_ext = jnp.concatenate([k_ref[pl.ds(r0, tq), pcols], ones_ext], axis=1)
            zs.append(lax.dot_general(qm_s[h], k_ext, nt, preferred_element_type=F32))
        sps, lbs = [], []
        for h in range(H):
            z = zs[h]
            sp = jnp.maximum(z, 0.0) + jnp.log2(1.0 + jnp.exp2(_neg_abs(z)))
            lbs.append(z - sp)
            if mask is not None:
                sp = jnp.where(mask, sp, 0.0)
            sps.append(sp)
        for h in range(H):
            sp = sps[h]
            later = jnp.dot(sp.astype(BF16), neg_upper, preferred_element_type=F32)
            p = jnp.exp2(lbs[h] + later)
            if mask is not None:
                p = jnp.where(mask, p, 0.0)
            p_s[h] = p.astype(BF16)
            rs_s[h] = jnp.sum(sp, axis=-1, keepdims=True)

    def accumulate(kb, first):
        r0 = pl.multiple_of(kb * tq, tq)
        for hp in range(H // 2):
            pcols = slice(hp * pw, (hp + 1) * pw)
            vp = v_ref[pl.ds(r0, tq), pcols]
            pv = []
            for h in (2 * hp, 2 * hp + 1):
                x = jnp.dot(p_s[h], vp, preferred_element_type=F32)
                if first:
                    c_s[h] = rs_s[h]
                else:
                    c = c_s[h]
                    x = jnp.exp2(-c) * x
                    c_s[h] = c + rs_s[h]
                pv.append(x)
            both = jnp.where(low_half, pv[0], pv[1])
            if first:
                o_ref[:, pcols] = both
            else:
                o_ref[:, pcols] += both

    weights(i, diag_mask)

    @pl.when(i > 0)
    def _():
        accumulate(i, True)
        weights(i - 1, None)

        def body(kk, carry):
            accumulate(i - kk + 1, False)
            weights(i - kk, None)
            return carry

        lax.fori_loop(2, i + 1, body, 0)
        accumulate(0, False)

    @pl.when(i == 0)
    def _():
        accumulate(0, True)


def sb_prompt(qkv, bias, nb, seq):
    tq = SB_TILE
    nq = seq // tq
    w = SB_WIDTH
    return pl.pallas_call(
        functools.partial(_sb_prompt_body, tq=tq), grid=(nb, nq),
        in_specs=[pl.BlockSpec(memory_space=pltpu.SMEM),
                  pl.BlockSpec((tq, w), lambda b, i: (b * nq + i, 0)),
                  pl.BlockSpec((seq, w), lambda b, i: (b, 1)),
                  pl.BlockSpec((seq, w), lambda b, i: (b, 2))],
        out_specs=pl.BlockSpec((tq, w), lambda b, i: (b * nq + i, 0)),
        out_shape=jax.ShapeDtypeStruct((nb * seq, w), F32),
        scratch_shapes=[pltpu.VMEM((SB_HEADS, tq, 1), F32), pltpu.VMEM((SB_HEADS, tq, 1), F32),
                        pltpu.VMEM((SB_HEADS, tq, tq), BF16),
                        pltpu.VMEM((SB_HEADS, tq, 4 * SB_HEAD_DIM), BF16)],
        compiler_params=_cp("parallel", "arbitrary"), name="sb_prompt")(bias, qkv, qkv, qkv)


def _sb_blocks_heads(q3, kts, vts, bias, neg_upper, mask):
    nh, nr, _ = q3.shape
    zs = [lax.dot_general(q3, kt, (((2,), (1,)), ((0,), (0,))), preferred_element_type=F32) + bias
          for kt in kts]
    sps, lbs = [], []
    for z in zs:
        sp = _softplus(z)
        lbs.append(z - sp)
        sps.append(sp if mask is None else jnp.where(mask, sp, 0.0))
    ps = []
    for sp, lb in zip(sps, lbs):
        nk = sp.shape[2]
        sp2 = sp.reshape(nh * nr, nk)
        hi = sp2.astype(BF16)
        lo = (sp2 - hi.astype(F32)).astype(BF16)
        later = (jnp.dot(hi, neg_upper, preferred_element_type=F32)
                 + jnp.dot(lo, neg_upper, preferred_element_type=F32)).reshape(nh, nr, nk)
        p = jnp.exp(lb + later)
        ps.append((p if mask is None else jnp.where(mask, p, 0.0)).astype(BF16))
    pvs = [lax.dot_general(p, vt, (((2,), (2,)), ((0,), (0,))), preferred_element_type=F32)
           for p, vt in zip(ps, vts)]
    return pvs, [jnp.sum(sp, axis=-1, keepdims=True) for sp in sps]


def _sb_sample_body(pt_ref, q_ref, bias_ref, kn_ref, vn_ref, ck_ref, cv_ref, o_ref,
                    kbuf, vbuf, sems, c_s, acc_s, *, npg, layer):
    b = pl.program_id(0)
    j = pl.program_id(1)
    nsteps = pl.num_programs(1)
    npages = nsteps * npg
    step = b * nsteps + j
    slot = lax.rem(step, 2)
    ps = PAGE_SIZE
    q3 = q_ref[...]
    bias = bias_ref[...]
    neg_upper = _neg_strict_upper(ps)

    def page_copies(seq, jj, sl, n):
        page = pt_ref[seq, npages - 1 - (jj * npg + n)]
        return (pltpu.make_async_copy(ck_ref.at[layer, page], kbuf.at[sl, n], sems.at[sl]),
                pltpu.make_async_copy(cv_ref.at[layer, page], vbuf.at[sl, n], sems.at[sl]))

    def fetch(seq, jj, sl):
        for n in range(npg):
            for cp in page_copies(seq, jj, sl, n):
                cp.start()

    @pl.when(step == 0)
    def _():
        fetch(b, j, slot)

    @pl.when(step + 1 < pl.num_programs(0) * nsteps)
    def _():
        last = j == nsteps - 1
        fetch(jnp.where(last, b + 1, b), jnp.where(last, 0, j + 1), 1 - slot)

    @pl.when(j == 0)
    def _():
        shape = q3.shape[:2] + (ps,)
        mask = lax.broadcasted_iota(jnp.int32, shape, 2) < lax.broadcasted_iota(jnp.int32, shape, 1)
        pvs, rss = _sb_blocks_heads(q3, [kn_ref[...].astype(BF16)], [vn_ref[...].astype(BF16)], bias,
                                    neg_upper, mask)
        acc_s[...] = pvs[0]
        c_s[...] = rss[0]

    for n in range(npg):
        for cp in page_copies(b, j, slot, n):
            cp.wait()
    pvs, rss = _sb_blocks_heads(q3, [kbuf[slot, n].astype(BF16) for n in range(npg)],
                                [vbuf[slot, n].astype(BF16) for n in range(npg)], bias, neg_upper, None)
    c = c_s[...]
    acc = acc_s[...]
    for pv, rs in zip(pvs, rss):
        acc = acc + jnp.exp(-c) * pv
        c = c + rs
    c_s[...] = c
    acc_s[...] = acc

    @pl.when(j == nsteps - 1)
    def _():
        o_ref[...] = acc


def sb_sample(q3, bias, k_new_t, v_new_t, cache_k_t, cache_v_t, page_table, layer):
    nseq, npages = page_table.shape
    npg = PAGES_PER_STEP
    nsteps = npages // npg
    _, nh, nr, hd = q3.shape
    per_seq = lambda r, c: pl.BlockSpec((None, nh, r, c), lambda b, j, pt: (b, 0, 0, 0))
    any_spec = pl.BlockSpec(memory_space=pl.ANY)
    grid_spec = pltpu.PrefetchScalarGridSpec(
        num_scalar_prefetch=1, grid=(nseq, nsteps),
        in_specs=[per_seq(nr, hd), pl.BlockSpec((nh, 1, 1), lambda b, j, pt: (0, 0, 0)),
                  per_seq(hd, PAGE_SIZE), per_seq(hd, PAGE_SIZE), any_spec, any_spec],
        out_specs=per_seq(nr, hd),
        scratch_shapes=[pltpu.VMEM((2, npg, nh, hd, PAGE_SIZE), F32), pltpu.VMEM((2, npg, nh, hd, PAGE_SIZE), F32),
                        pltpu.SemaphoreType.DMA((2,)),
                        pltpu.VMEM((nh, nr, 1), F32), pltpu.VMEM((nh, nr, hd), F32)])
    return pl.pallas_call(
        functools.partial(_sb_sample_body, npg=npg, layer=layer), grid_spec=grid_spec,
        out_shape=jax.ShapeDtypeStruct((nseq, nh, nr, hd), F32),
        compiler_params=_cp("arbitrary", "arbitrary"), name="sb_sample")(
            page_table, q3, bias, k_new_t, v_new_t, cache_k_t, cache_v_t)


def _pad_prev(buf):
    return jnp.pad(buf, ((0, 0), (SUBLANES - (CONV_W - 1), 0), (0, 0)))


def _prep_weights(W):
    P = {}
    w_in = W['w_in']
    o = [0, 256, 512, 1024, 1536, 2048, 2304, 2816, 3328]
    P['w_in'] = jnp.concatenate(
        [w_in[:, :, o[0]:o[5]], w_in[:, :, o[6]:o[7]], w_in[:, :, o[5]:o[6]],
         jnp.pad(w_in[:, :, o[7]:o[8]], ((0, 0), (0, 0), (0, C_END - C_DT - SSD_HEADS)))], axis=-1).astype(BF16)
    eye = jnp.eye(LRU_BLOCKS, dtype=F32)
    bd = lambda w: jnp.einsum('lgij,gh->lgihj', w, eye).reshape(DEPTH, LRU_WIDTH, LRU_WIDTH)
    P['lru_wgate'] = jnp.concatenate([bd(W['lru_wa']), bd(W['lru_wx'])], axis=-1).astype(BF16)
    P['lru_bgate'] = jnp.concatenate([W['lru_ba'], W['lru_bx']], axis=-1)[:, None, :]
    pad_cw = lambda w: jnp.pad(w, ((0, 0), (0, SUBLANES - CONV_W), (0, 0)))
    P['lru_cw'] = pad_cw(W['lru_conv_w'])
    P['ssd_cw'] = pad_cw(W['ssd_conv_w'])
    lane_pad = lambda v: jnp.pad(v, ((0, 0), (0, LANES - SSD_HEADS)))[:, None, :]
    P['ssd_dtb'] = lane_pad(W['ssd_dt_bias'])
    P['ssd_a'] = lane_pad(-jnp.exp(W['ssd_a_log']))
    P['ssd_d'] = jnp.repeat(W['ssd_d'], SSD_HEAD_DIM, axis=-1)[:, None, :]
    for name in ('w_out', 'ffn_w_gate', 'ffn_w_up', 'ffn_w_down', 'moe_w_gate', 'moe_w_up', 'moe_w_down',
                 'ple_w_gate', 'ple_w_proj'):
        P[name] = W[name].astype(BF16)
    return P


def _run_group(x, p, nb, seq, valid_len, at_pos0, lru_h0, lru_buf0, ssd_s0, ssd_buf0, sb_fn, W, P):
    t = nb * seq
    tm = min(512, t)
    tl = min(256, seq)
    rows = min(SSD_CHUNK, seq)
    vl = None if valid_len == seq else valid_len
    kv_in_kernel = valid_len == seq and seq % min(256, t) == 0
    states = []
    for li in range(DEPTH):
        ln = (W['ln_in_g'], W['ln_in_b']) if li == 0 else None
        tp = min(256, t)
        if kv_in_kernel:
            res = in_proj(x, P['w_in'][li], tp, li, ln=ln, kv=None if li == 0 else (kt_all, vt_all), seq=seq)
            kt_all, vt_all = res[-2:]
            res = res[:-2]
        else:
            res = in_proj(x, P['w_in'][li], tp, li, ln=ln)
        if li == 0:
            x = res[0]
        proj, qkv16 = res[-2:]
        y_lru, lru_ht = lru_mixer(proj, nb, seq, tl, _pad_prev(lru_buf0[li]), lru_h0[li][:, None, :],
                                  P['lru_cw'][li], W['lru_conv_b'][li][None], P['lru_wgate'][li],
                                  P['lru_bgate'][li], W['lru_lambda'][li][None], at_pos0, vl)
        y_ssd, ssd_st = ssd_mixer(proj, nb, seq, rows, _pad_prev(ssd_buf0[li]), ssd_s0[li],
                                  P['ssd_cw'][li], W['ssd_conv_b'][li][None], P['ssd_dtb'][li], P['ssd_a'][li],
                                  P['ssd_d'][li], W['ssd_norm_w'][li][None], vl)
        y_sb = sb_fn(li, proj, qkv16)
        x1 = outproj_ln1(y_lru, y_sb, y_ssd, x, P['w_out'][li], W['ln1_g'][li], W['ln1_b'][li], tm)
        epilogue = (p[li], P['ple_w_gate'][li], P['ple_w_proj'][li], W['ln2_g'][li][None], W['ln2_b'][li][None])
        j = li // 2
        if li % 2 == 0:
            x = ffn_dense(x1, P['ffn_w_gate'][j], P['ffn_w_up'][j], P['ffn_w_down'][j], epilogue, tm, 1408)
        else:
            gates, sel, tile_counts = router_gates(x1, W['moe_router'][j], tm)
            if t >= MOE_SPARSE_MIN_TOKENS:
                x = moe_sparse(x1, sel, tile_counts, P['moe_w_gate'][j], P['moe_w_up'][j], P['moe_w_down'][j],
                               tm, 1792, epilogue)
            else:
                f = moe_dense(x1, gates, P['moe_w_gate'][j], P['moe_w_up'][j], P['moe_w_down'][j], tm, 1792)
                x = ple_ln2(x1, f, epilogue, tm)
        p3 = proj.reshape(nb, seq, C_END)
        tail = slice(valid_len - (CONV_W - 1), valid_len)
        heads = lambda c: p3[:, :valid_len, c:c + SB_WIDTH].reshape(nb, valid_len, SB_HEADS, SB_HEAD_DIM)
        states.append((None if kv_in_kernel else heads(C_K), None if kv_in_kernel else heads(C_V),
                       lru_ht[:, 0], p3[:, tail, C_XL:C_XL + LRU_WIDTH], ssd_st,
                       p3[:, tail, C_XBC:C_XBC + SSD_CONV_DIM]))
    stacked = tuple(jnp.stack([st[n] for st in states]) for n in range(2 if kv_in_kernel else 0, 6))
    if kv_in_kernel:
        seq_last = lambda a: jnp.transpose(a.reshape(DEPTH, nb, SB_HEADS, SB_HEAD_DIM, seq), (0, 1, 4, 2, 3))
        stacked = (seq_last(kt_all), seq_last(vt_all)) + stacked
    return x, stacked


def kernel(x_prompt, x_sample, p_prompt, p_sample, cache_k, cache_v, page_table, state_lru_h, state_lru_conv, state_ssd, state_ssd_conv, ln_in_g, ln_in_b, w_in, lru_conv_w, lru_conv_b, lru_wa, lru_ba, lru_wx, lru_bx, lru_lambda, sb_bias, ssd_conv_w, ssd_conv_b, ssd_dt_bias, ssd_a_log, ssd_d, ssd_norm_w, w_out, ln1_g, ln1_b, ln2_g, ln2_b, ffn_w_gate, ffn_w_up, ffn_w_down, moe_router, moe_w_gate, moe_w_up, moe_w_down, ple_w_gate, ple_w_proj):
    W = dict(ln_in_g=ln_in_g, ln_in_b=ln_in_b, w_in=w_in, lru_conv_w=lru_conv_w, lru_conv_b=lru_conv_b,
             lru_wa=lru_wa, lru_ba=lru_ba, lru_wx=lru_wx, lru_bx=lru_bx, lru_lambda=lru_lambda,
             sb_bias=sb_bias, ssd_conv_w=ssd_conv_w, ssd_conv_b=ssd_conv_b, ssd_dt_bias=ssd_dt_bias,
             ssd_a_log=ssd_a_log, ssd_d=ssd_d, ssd_norm_w=ssd_norm_w, w_out=w_out, ln1_g=ln1_g,
             ln1_b=ln1_b, ln2_g=ln2_g, ln2_b=ln2_b, ffn_w_gate=ffn_w_gate, ffn_w_up=ffn_w_up,
             ffn_w_down=ffn_w_down, moe_router=moe_router, moe_w_gate=moe_w_gate, moe_w_up=moe_w_up,
             moe_w_down=moe_w_down, ple_w_gate=ple_w_gate, ple_w_proj=ple_w_proj)
    P = _prep_weights(W)
    d = D_MODEL

    bp, sp, _ = x_prompt.shape
    zeros = lambda *s: jnp.zeros((DEPTH, bp) + s, F32)

    def prompt_sb(li, proj, qkv16):
        del proj
        return sb_prompt(qkv16, sb_bias[li], bp, sp)

    y_prompt, st_p = _run_group(
        x_prompt.reshape(bp * sp, d), p_prompt.reshape(DEPTH, bp * sp, PLE_DIM), bp, sp, sp, True,
        zeros(LRU_WIDTH), zeros(CONV_W - 1, LRU_WIDTH), zeros(SSD_HEADS, SSD_HEAD_DIM, SSD_D_STATE),
        zeros(CONV_W - 1, SSD_CONV_DIM), prompt_sb, W, P)

    bs, ts, _ = x_sample.shape
    rs = SAMPLE_ROWS
    pad_rows = lambda a, ax: jnp.pad(a, [(0, rs - ts) if n == ax else (0, 0) for n in range(a.ndim)])
    scale = 1.0 / math.sqrt(SB_HEAD_DIM)
    cache_k_t = jnp.transpose(cache_k, (0, 1, 3, 4, 2))
    cache_v_t = jnp.transpose(cache_v, (0, 1, 3, 4, 2))
    row_valid = (jnp.arange(rs) < ts)[None, None, :, None]

    def sample_sb(li, proj, qkv16):
        del qkv16
        p3 = proj.reshape(bs, rs, C_END)
        heads = lambda c: p3[:, :, c:c + SB_WIDTH].reshape(bs, rs, SB_HEADS, SB_HEAD_DIM)
        q3 = jnp.where(row_valid, jnp.transpose(heads(C_Q), (0, 2, 1, 3)) * scale, 0.0).astype(BF16)
        new_t = lambda c: jnp.pad(jnp.transpose(heads(c), (0, 2, 3, 1)), ((0, 0),) * 3 + ((0, PAGE_SIZE - rs),))
        out = sb_sample(q3, sb_bias[li][:, None, None], new_t(C_K), new_t(C_V), cache_k_t, cache_v_t,
                        page_table, li)
        return jnp.transpose(out, (0, 2, 1, 3)).reshape(bs * rs, SB_WIDTH)

    y_sample, st_s = _run_group(
        pad_rows(x_sample, 1).reshape(bs * rs, d), pad_rows(p_sample, 2).reshape(DEPTH, bs * rs, PLE_DIM),
        bs, rs, ts, False, state_lru_h, state_lru_conv, state_ssd, state_ssd_conv, sample_sb, W, P)
    y_sample = y_sample.reshape(bs, rs, d)[:, :ts]

    return (y_prompt.reshape(bp, sp, d), y_sample) + st_p + st_s
```
